```python
import jax, jax.numpy as jnp
from jax import lax
import numpy as np

D_MODEL = 1024
BATCH = 32
SEQ = 256
DEPTH = 4
DEC_BATCH = 8
DEC_SEQ = 4096
PAST_LEN = 512

GRID_W = 64
A_WIDTH = D_MODEL // 2
A_CHUNK = 128
A_GROUPS = 4
A_GD = A_WIDTH // A_GROUPS
DN_HEADS = 4
DN_DK = 128
DN_DV = 128
DN_QK = DN_HEADS * DN_DK
DN_VW = DN_HEADS * DN_DV
DN_CONV_K = 5
DN_CHUNK = 64
MLA_HEADS = 4
MLA_NOPE = 128
MLA_ROPE = 64
MLA_V = 128
Q_LORA = D_MODEL // 4
KV_LORA = D_MODEL // 4
AXIS_FREQS = MLA_ROPE // 4
ROPE_BASE = 10000.0
Q_BLOCK = 128
D_FF = 256 * ((8 * D_MODEL // 3 + 255) // 256)
N_EXPERTS = 8
TOP_K = 2
D_FF_EXPERT = D_FF // 2
N_DENSE = (DEPTH + 1) // 2
N_MOE = DEPTH // 2
DN_ALPHA = (2 * DEPTH) ** 0.25
DN_BETA = (8 * DEPTH) ** -0.25
EPS = 1e-6
IN_SPLITS = (A_WIDTH, A_WIDTH,
             DN_QK, DN_QK, DN_VW, DN_VW, DN_HEADS, DN_HEADS, DN_HEADS, DN_HEADS,
             Q_LORA, KV_LORA, MLA_ROPE,
             D_MODEL, D_MODEL, D_MODEL)
N_IN = sum(IN_SPLITS)

kernel_name = 'hybrid_gmlp_gdn_mla_prefix_diffusion_step'


def split_cols(x, sizes):
    cuts, acc = [], 0
    for s in sizes[:-1]:
        acc += s
        cuts.append(acc)
    return jnp.split(x, cuts, axis=-1)


def layer_norm(x, g, b):
    xf = x.astype(jnp.float32)
    xc = xf - jnp.mean(xf, -1, keepdims=True)
    var = jnp.mean(xc * xc, -1, keepdims=True)
    return (xc * lax.rsqrt(var + 1e-5) * g + b).astype(x.dtype)


def rms_norm(x, g):
    xf = x.astype(jnp.float32)
    return (xf * lax.rsqrt(jnp.mean(xf * xf, -1, keepdims=True) + EPS) * g).astype(x.dtype)


def l2_normalize(x):
    xf = x.astype(jnp.float32)
    return xf * lax.rsqrt(jnp.sum(xf * xf, -1, keepdims=True) + EPS)


def adaln(cond, w, b):
    return jnp.split(jax.nn.silu(cond) @ w + b, 6, axis=-1)


def axial_rope_tables(n_tok):
    n_rows = n_tok // GRID_W
    row = jnp.repeat(jnp.arange(n_rows, dtype=jnp.float32), GRID_W)
    col = (jnp.arange(n_tok) % GRID_W).astype(jnp.float32)
    inv_freq = ROPE_BASE ** (-jnp.arange(AXIS_FREQS, dtype=jnp.float32) / AXIS_FREQS)
    ang = jnp.stack([row[:, None] * inv_freq, col[:, None] * inv_freq], axis=1)
    return jnp.cos(ang), jnp.sin(ang)


def apply_axial_rope(x, cos, sin):
    shp = cos.shape[:1] + (1,) * (x.ndim - 3) + cos.shape[1:]
    cos, sin = cos.reshape(shp), sin.reshape(shp)
    xa = x.astype(jnp.float32).reshape(x.shape[:-1] + (2, 2, AXIS_FREQS))
    x1, x2 = xa[..., 0, :], xa[..., 1, :]
    out = jnp.stack([x1 * cos - x2 * sin, x2 * cos + x1 * sin], axis=-2)
    return out.reshape(x.shape).astype(x.dtype)


def chunk_spatial_gating(u, v, w_s, b_s, ln_g, ln_b):
    bsz, n_tok, _ = v.shape
    vn = layer_norm(v, ln_g, ln_b).reshape(bsz, n_tok // A_CHUNK, A_CHUNK, A_GROUPS, A_GD)
    mixed = jnp.einsum('gpq,bnqgc->bnpgc', w_s, vn) + b_s.T[:, :, None]
    return u * mixed.reshape(bsz, n_tok, A_WIDTH)


def short_conv(x, w):
    ch = x.shape[-1]
    y = lax.conv_general_dilated(x, w.astype(x.dtype)[:, None, :], (1,),
                                 [(DN_CONV_K // 2, DN_CONV_K // 2)],
                                 dimension_numbers=('NWC', 'WIO', 'NWC'),
                                 feature_group_count=ch)
    return jax.nn.silu(y)


def gated_delta_rule(q, k, v, g, beta, s0):
    bsz, n_tok, n_h, dk = q.shape
    dv = v.shape[-1]
    n_ch = n_tok // DN_CHUNK

    def to_chunks(t):
        t = t.astype(jnp.float32).reshape(bsz, n_ch, DN_CHUNK, n_h, -1)
        return t.transpose(1, 0, 3, 2, 4)

    qc = to_chunks(q) * (dk ** -0.5)
    kc, vc = to_chunks(k), to_chunks(v)
    gc = to_chunks(g[..., None])[..., 0]
    bc = to_chunks(beta[..., None])[..., 0]
    g_cum = jnp.cumsum(gc, axis=-1)
    incl = jnp.tril(jnp.ones((DN_CHUNK, DN_CHUNK), dtype=bool))
    strict = jnp.tril(jnp.ones((DN_CHUNK, DN_CHUNK), dtype=bool), -1)
    diff = g_cum[..., :, None] - g_cum[..., None, :]
    decay = jnp.where(incl, jnp.exp(jnp.where(incl, diff, 0.0)), 0.0)
    k_beta = kc * bc[..., None]
    lower = jnp.where(strict, jnp.einsum('nbhid,nbhjd->nbhij', k_beta, kc) * decay, 0.0)
    eye = jnp.eye(DN_CHUNK, dtype=jnp.float32)
    rhs = jnp.concatenate([vc * bc[..., None], k_beta * jnp.exp(g_cum)[..., None]], axis=-1)
    sol = lax.linalg.triangular_solve(lower + eye, rhs, left_side=True, lower=True, unit_diagonal=True)
    u, w = sol[..., :dv], sol[..., dv:]
    qk = jnp.where(incl, jnp.einsum('nbhid,nbhjd->nbhij', qc, kc) * decay, 0.0)

    def step(state, xs):
        q_i, k_i, u_i, w_i, g_i, a_i = xs
        v_new = u_i - w_i @ state
        o_i = (q_i * jnp.exp(g_i)[..., None]) @ state + a_i @ v_new
        g_last = g_i[..., -1]
        k_dec = k_i * jnp.exp(g_last[..., None] - g_i)[..., None]
        state = state * jnp.exp(g_last)[..., None, None] + jnp.einsum('bhck,bhcv->bhkv', k_dec, v_new)
        return state, o_i

    s_final, o = lax.scan(step, s0.astype(jnp.float32), (qc, kc, u, w, g_cum, qk))
    o = o.transpose(1, 0, 3, 2, 4).reshape(bsz, n_tok, n_h, dv)
    return o, s_final


def blocked_attention(q, k, v):
    bsz, n_q, n_h, dh = q.shape
    qb = q.reshape(bsz, n_q // Q_BLOCK, Q_BLOCK, n_h, dh).transpose(1, 0, 2, 3, 4)
    scale = dh ** -0.5

    def one_block(q_blk):
        s = jnp.einsum('bqhd,bkhd->bhqk', q_blk, k).astype(jnp.float32) * scale
        p = jax.nn.softmax(s, axis=-1)
        return jnp.einsum('bhqk,bkhd->bqhd', p.astype(v.dtype), v)

    o = lax.map(one_block, qb)
    return o.transpose(1, 0, 2, 3, 4).reshape(bsz, n_q, n_h, v.shape[-1])


def token_mixer(h, p, rope, ctx):
    bsz, n_tok, _ = h.shape
    (a_u, a_v, dq, dk_, dv_, dz, da_f, da_b, db_f, db_b,
     cq, ckv, kr, gate_a, gate_b, gate_c) = split_cols(h @ p['w_in'], IN_SPLITS)

    o_a = chunk_spatial_gating(jax.nn.gelu(a_u), jax.nn.gelu(a_v), p['a_ws'], p['a_bs'], p['a_ln_g'], p['a_ln_b'])

    qkv = short_conv(jnp.concatenate([dq, dk_, dv_], axis=-1), p['dn_conv'])
    dq, dk_, dv_ = split_cols(qkv, (DN_QK, DN_QK, DN_VW))
    q = l2_normalize(dq.reshape(bsz, n_tok, DN_HEADS, DN_DK))
    k = l2_normalize(dk_.reshape(bsz, n_tok, DN_HEADS, DN_DK))
    v = dv_.reshape(bsz, n_tok, DN_HEADS, DN_DV)
    a_log, dt_bias = p['dn_a_log'], p['dn_dt_bias']
    g_f = -jnp.exp(a_log[0]) * jax.nn.softplus(da_f.astype(jnp.float32) + dt_bias[0])
    g_b = -jnp.exp(a_log[1]) * jax.nn.softplus(da_b.astype(jnp.float32) + dt_bias[1])
    beta_f = jax.nn.sigmoid(db_f.astype(jnp.float32))
    beta_b = jax.nn.sigmoid(db_b.astype(jnp.float32))
    if ctx is None:
        s_f0 = jnp.zeros((bsz, DN_HEADS, DN_DK, DN_DV), jnp.float32)
        s_b0 = s_f0
    else:
        s_f0, s_b0 = ctx[0], ctx[1]
    o_f, s_f = gated_delta_rule(q, k, v, g_f, beta_f, s_f0)
    o_r, s_b = gated_delta_rule(jnp.flip(q, 1), jnp.flip(k, 1), jnp.flip(v, 1),
                                jnp.flip(g_b, 1), jnp.flip(beta_b, 1), s_b0)
    o_dn = rms_norm(o_f + jnp.flip(o_r, 1), p['dn_norm']) * jax.nn.silu(dz.reshape(bsz, n_tok, DN_HEADS, DN_DV))
    o_b = o_dn.reshape(bsz, n_tok, DN_VW)

    q_c = (rms_norm(cq, p['q_norm']) @ p['w_qb']).reshape(bsz, n_tok, MLA_HEADS, MLA_NOPE + MLA_ROPE)
    q_nope, q_rope = q_c[..., :MLA_NOPE], q_c[..., MLA_NOPE:]
    ckv_n = rms_norm(ckv, p['kv_norm'])
    if rope is None:
        c_all, kr_all = ckv_n, kr
    else:
        cos, sin = rope
        q_rope = apply_axial_rope(q_rope, cos, sin)
        c_all = jnp.concatenate([ctx[2], ckv_n], axis=1)
        kr_all = jnp.concatenate([ctx[3], apply_axial_rope(kr, cos, sin)], axis=1)
    kv = (c_all @ p['w_kvb']).reshape(bsz, c_all.shape[1], MLA_HEADS, MLA_NOPE + MLA_V)
    k_full = jnp.concatenate(
        [kv[..., :MLA_NOPE],
         jnp.broadcast_to(kr_all[:, :, None, :], (bsz, c_all.shape[1], MLA_HEADS, MLA_ROPE))], axis=-1)
    q_full = jnp.concatenate([q_nope, q_rope], axis=-1)
    o_c = blocked_attention(q_full, k_full, kv[..., MLA_NOPE:]).reshape(bsz, n_tok, MLA_HEADS * MLA_V)

    merged = (jax.nn.sigmoid(gate_a) * (o_a @ p['w_pa'])
              + jax.nn.sigmoid(gate_b) * (o_b @ p['w_pb'])
              + jax.nn.sigmoid(gate_c) * (o_c @ p['w_pc']))
    return merged @ p['w_out'], (s_f, s_b, ckv_n, kr)


def swiglu(x, w_gu, w_down):
    gate, up = jnp.split(x @ w_gu, 2, axis=-1)
    return (jax.nn.silu(gate) * up) @ w_down


def moe_swiglu(x, w_router, b_router, w_gu, w_down):
    logits = (x @ w_router).astype(jnp.float32)
    _, top_i = lax.top_k(logits + b_router, TOP_K)
    top_w = jax.nn.softmax(jnp.take_along_axis(logits, top_i, axis=-1), axis=-1)
    combine = jnp.sum(jax.nn.one_hot(top_i, N_EXPERTS, dtype=jnp.float32) * top_w[..., None], axis=-2)
    out = jnp.zeros_like(x)
    for e in range(N_EXPERTS):
        out = out + combine[..., e:e + 1].astype(x.dtype) * swiglu(x, w_gu[e], w_down[e])
    return out


def channel_mixer(h, l, ffn_gu, ffn_down, moe_router, moe_bias, moe_gu, moe_down):
    if l % 2 == 0:
        return swiglu(h, ffn_gu[l // 2], ffn_down[l // 2])
    return moe_swiglu(h, moe_router[l // 2], moe_bias[l // 2], moe_gu[l // 2], moe_down[l // 2])


def setup_inputs(seed: int = 0) -> dict:
    key = jax.random.key(seed)
    keys = list(jax.random.split(key, 48))

    def nrm(shape, scale):
        return scale * jax.random.normal(keys.pop(), shape, jnp.float32)

    def unif(shape, lo, hi):
        return jax.random.uniform(keys.pop(), shape, jnp.float32, lo, hi)

    dt = jnp.exp(unif((DEPTH, 2, DN_HEADS), float(np.log(1e-3)), float(np.log(1e-1))))
    return {
        'x_prompt': nrm((BATCH, SEQ, D_MODEL), 1.0),
        'x_sample': nrm((DEC_BATCH, DEC_SEQ, D_MODEL), 1.0),
        'state_dn': nrm((DEC_BATCH, DEPTH, 2, DN_HEADS, DN_DK, DN_DV), 0.05),
        'cache_ckv': nrm((DEC_BATCH, DEPTH, PAST_LEN, KV_LORA), 1.0),
        'cache_krope': nrm((DEC_BATCH, DEPTH, PAST_LEN, MLA_ROPE), 1.0),
        'c': nrm((DEC_BATCH, D_MODEL), 1.0),
        'c_ctx': nrm((D_MODEL,), 1.0),
        'w_mod': nrm((DEPTH, D_MODEL, 6 * D_MODEL), D_MODEL ** -0.5),
        'b_mod': nrm((DEPTH, 6 * D_MODEL), 0.02),
        'w_in': nrm((DEPTH, D_MODEL, N_IN), D_MODEL ** -0.5),
        'a_ln_g': 1.0 + nrm((DEPTH, A_WIDTH), 0.02),
        'a_ln_b': nrm((DEPTH, A_WIDTH), 0.02),
        'a_ws': nrm((DEPTH, A_GROUPS, A_CHUNK, A_CHUNK), A_CHUNK ** -0.5),
        'a_bs': 1.0 + nrm((DEPTH, A_GROUPS, A_CHUNK), 0.02),
        'dn_conv': nrm((DEPTH, DN_CONV_K, 2 * DN_QK + DN_VW), DN_CONV_K ** -0.5),
        'dn_a_log': jnp.log(unif((DEPTH, 2, DN_HEADS), 1.0, 16.0)),
        'dn_dt_bias': jnp.log(jnp.expm1(dt)),
        'dn_norm': 1.0 + nrm((DEPTH, DN_DV), 0.02),
        'q_norm': 1.0 + nrm((DEPTH, Q_LORA), 0.02),
        'w_qb': nrm((DEPTH, Q_LORA, MLA_HEADS * (MLA_NOPE + MLA_ROPE)), Q_LORA ** -0.5),
        'kv_norm': 1.0 + nrm((DEPTH, KV_LORA), 0.02),
        'w_kvb': nrm((DEPTH, KV_LORA, MLA_HEADS * (MLA_NOPE + MLA_V)), KV_LORA ** -0.5),
        'w_pa': nrm((DEPTH, A_WIDTH, D_MODEL), A_WIDTH ** -0.5),
        'w_pb': nrm((DEPTH, DN_VW, D_MODEL), DN_VW ** -0.5),
        'w_pc': nrm((DEPTH, MLA_HEADS * MLA_V, D_MODEL), (MLA_HEADS * MLA_V) ** -0.5),
        'w_out': nrm((DEPTH, D_MODEL, D_MODEL), D_MODEL ** -0.5 * DN_BETA),
        'ln_g': 1.0 + nrm((DEPTH, 2, D_MODEL), 0.02),
        'ln_b': nrm((DEPTH, 2, D_MODEL), 0.02),
        'ffn_gu': nrm((N_DENSE, D_MODEL, 2 * D_FF), D_MODEL ** -0.5),
        'ffn_down': nrm((N_DENSE, D_FF, D_MODEL), D_FF ** -0.5 * DN_BETA),
        'moe_router': nrm((N_MOE, D_MODEL, N_EXPERTS), D_MODEL ** -0.5),
        'moe_bias': nrm((N_MOE, N_EXPERTS), 0.01),
        'moe_gu': nrm((N_MOE, N_EXPERTS, D_MODEL, 2 * D_FF_EXPERT), D_MODEL ** -0.5),
        'moe_down': nrm((N_MOE, N_EXPERTS, D_FF_EXPERT, D_MODEL), D_FF_EXPERT ** -0.5 * DN_BETA),
    }


def reference(x_prompt, x_sample, state_dn, cache_ckv, cache_krope, c, c_ctx, w_mod, b_mod, w_in,
              a_ln_g, a_ln_b, a_ws, a_bs, dn_conv, dn_a_log, dn_dt_bias, dn_norm, q_norm, w_qb,
              kv_norm, w_kvb, w_pa, w_pb, w_pc, w_out, ln_g, ln_b, ffn_gu, ffn_down,
              moe_router, moe_bias, moe_gu, moe_down):
    rope = axial_rope_tables(x_sample.shape[1])
    xp, xs = x_prompt, x_sample
    st_f, st_b, st_ckv, st_kr = [], [], [], []
    for l in range(DEPTH):
        p = {'w_in': w_in[l], 'a_ln_g': a_ln_g[l], 'a_ln_b': a_ln_b[l], 'a_ws': a_ws[l], 'a_bs': a_bs[l],
             'dn_conv': dn_conv[l], 'dn_a_log': dn_a_log[l], 'dn_dt_bias': dn_dt_bias[l], 'dn_norm': dn_norm[l],
             'q_norm': q_norm[l], 'w_qb': w_qb[l], 'kv_norm': kv_norm[l], 'w_kvb': w_kvb[l],
             'w_pa': w_pa[l], 'w_pb': w_pb[l], 'w_pc': w_pc[l], 'w_out': w_out[l]}

        sh1, sc1, g1, sh2, sc2, g2 = adaln(c_ctx, w_mod[l], b_mod[l])
        mix, (s_f, s_b, ckv_n, kr) = token_mixer(xp * (1 + sc1) + sh1, p, None, None)
        xp = layer_norm(DN_ALPHA * xp + g1 * mix, ln_g[l, 0], ln_b[l, 0])
        ff = channel_mixer(xp * (1 + sc2) + sh2, l, ffn_gu, ffn_down, moe_router, moe_bias, moe_gu, moe_down)
        xp = layer_norm(DN_ALPHA * xp + g2 * ff, ln_g[l, 1], ln_b[l, 1])
        st_f.append(s_f)
        st_b.append(s_b)
        st_ckv.append(ckv_n)
        st_kr.append(kr)

        sh1, sc1, g1, sh2, sc2, g2 = adaln(c[:, None, :], w_mod[l], b_mod[l])
        ctx = (state_dn[:, l, 0], state_dn[:, l, 1], cache_ckv[:, l], cache_krope[:, l])
        mix, _ = token_mixer(xs * (1 + sc1) + sh1, p, rope, ctx)
        xs = layer_norm(DN_ALPHA * xs + g1 * mix, ln_g[l, 0], ln_b[l, 0])
        ff = channel_mixer(xs * (1 + sc2) + sh2, l, ffn_gu, ffn_down, moe_router, moe_bias, moe_gu, moe_down)
        xs = layer_norm(DN_ALPHA * xs + g2 * ff, ln_g[l, 1], ln_b[l, 1])

    new_state_dn = jnp.stack([jnp.stack([f, b], axis=1) for f, b in zip(st_f, st_b)], axis=1)
    new_cache_ckv = jnp.stack(st_ckv, axis=1)
    new_cache_krope = jnp.stack(st_kr, axis=1)
    return (xp, xs, new_state_dn, new_cache_ckv, new_cache_krope)
```

```python
import functools
import math

import jax
import jax.numpy as jnp
from jax import lax
from jax.experimental import pallas as pl
from jax.experimental.pallas import tpu as pltpu

F32 = jnp.float32
BF16 = jnp.bfloat16

D_MODEL = 1024
DEPTH = 4
GRID_W = 64
A_WIDTH = 512
A_CHUNK = 128
A_GROUPS = 4
DN_HEADS = 4
DN_DK = 128
DN_DV = 128
DN_QK = DN_HEADS * DN_DK
DN_VW = DN_HEADS * DN_DV
DN_CONV_K = 5
DN_CHUNK = 64
MLA_HEADS = 4
MLA_NOPE = 128
MLA_ROPE = 64
MLA_V = 128
Q_LORA = 256
KV_LORA = 256
AXIS_FREQS = MLA_ROPE // 4
ROPE_BASE = 10000.0
D_FF = 2816
N_EXPERTS = 8
D_FF_EXPERT = D_FF // 2
DN_ALPHA = (2 * DEPTH) ** 0.25
EPS = 1e-6
LN_EPS = 1e-5

LANES = 128
HALO = 16
VMEM_LIMIT = 56 * 1024 * 1024

GATE0 = 0
QKV0 = 3 * D_MODEL
DZ0 = QKV0 + 3 * DN_QK
AU0 = DZ0 + DN_VW
AV0 = AU0 + A_WIDTH
CQ0 = AV0 + A_WIDTH
CKV0 = CQ0 + Q_LORA
KR0 = CKV0 + KV_LORA
SM0 = KR0 + LANES
NP = SM0 + LANES
QKV_W = 3 * DN_QK


def _cparams(sem):
    return pltpu.CompilerParams(dimension_semantics=sem, vmem_limit_bytes=VMEM_LIMIT)


def _dot(a, b):
    return jnp.dot(a, b, preferred_element_type=F32)


def _dot_nt(a, b):
    return lax.dot_general(a, b, (((1,), (1,)), ((), ())), preferred_element_type=F32)


def _dot_tn(a, b):
    return lax.dot_general(a, b, (((0,), (0,)), ((), ())), preferred_element_type=F32)


def _split3(x):
    hi = x.astype(BF16)
    r = x - hi.astype(F32)
    mid = r.astype(BF16)
    lo = (r - mid.astype(F32)).astype(BF16)
    return hi, mid, lo


def _dot_f32(a, b):
    ah, am, al = _split3(a)
    bh, bm, bl = _split3(b)
    return (_dot(ah, bh) + (_dot(ah, bm) + _dot(am, bh))
            + (_dot(am, bm) + _dot(ah, bl) + _dot(al, bh)))


def _layer_norm(r, g, b):
    mu = jnp.mean(r, -1, keepdims=True)
    rc = r - mu
    var = jnp.mean(rc * rc, -1, keepdims=True)
    return rc * lax.rsqrt(var + LN_EPS) * g + b


def _adaln_kernel(c_ref, w_ref, b_ref, o_ref):
    s = jax.nn.silu(c_ref[...])
    o_ref[...] = _dot_f32(s, w_ref[...]) + b_ref[...]


def _adaln(cond, w_mod, b_mod):
    rows = cond.shape[0]
    n_out = w_mod.shape[-1]
    tn = 1536
    return pl.pallas_call(
        _adaln_kernel,
        grid=(DEPTH, n_out // tn),
        in_specs=[
            pl.BlockSpec((rows, D_MODEL), lambda l, j: (0, 0)),
            pl.BlockSpec((None, D_MODEL, tn), lambda l, j: (l, 0, j)),
            pl.BlockSpec((None, 1, tn), lambda l, j: (l, 0, j)),
        ],
        out_specs=pl.BlockSpec((None, rows, tn), lambda l, j: (l, 0, j)),
        out_shape=jax.ShapeDtypeStruct((DEPTH, rows, n_out), F32),
        compiler_params=_cparams(("parallel", "parallel")),
        name="adaln",
    )(cond, w_mod, b_mod.reshape(DEPTH, 1, n_out))


def _inproj_kernel(x_ref, mod_ref, w_ref, y_ref, h_ref):
    @pl.when(pl.program_id(2) == 0)
    def _():
        sh = mod_ref[0, 0:1, :]
        sc = mod_ref[0, 1:2, :]
        h_ref[...] = (x_ref[0] * (1.0 + sc) + sh).astype(BF16)

    y_ref[0] = _dot(h_ref[...], w_ref[...]).astype(y_ref.dtype)


def _inproj(x, mod, w_in_p, layer):
    bm, lm, _ = x.shape
    tm, tn = 1024, 1152
    return pl.pallas_call(
        _inproj_kernel,
        grid=(bm, lm // tm, NP // tn),
        in_specs=[
            pl.BlockSpec((1, tm, D_MODEL), lambda b, i, j: (b, i, 0)),
            pl.BlockSpec((1, 6, D_MODEL), lambda b, i, j: (b, 0, 0)),
            pl.BlockSpec((None, D_MODEL, tn), lambda b, i, j: (layer, 0, j)),
        ],
        out_specs=pl.BlockSpec((1, tm, tn), lambda b, i, j: (b, i, j)),
        out_shape=jax.ShapeDtypeStruct((bm, lm, NP), BF16),
        scratch_shapes=[pltpu.VMEM((tm, D_MODEL), BF16)],
        compiler_params=_cparams(("parallel", "parallel", "arbitrary")),
        name="inproj",
    )(x, mod, w_in_p)


def _gmlp_kernel(u_ref, v_ref, ws_ref, bs_ref, g_ref, b_ref, o_ref):
    tl = u_ref.shape[0]
    gd = A_WIDTH // A_GROUPS
    v = jax.nn.gelu(v_ref[...].astype(F32))
    vn = _layer_norm(v, g_ref[...], b_ref[...]).astype(BF16)
    u = jax.nn.gelu(u_ref[...].astype(F32))
    for n in range(tl // A_CHUNK):
        rows = slice(n * A_CHUNK, (n + 1) * A_CHUNK)
        for g in range(A_GROUPS):
            cols = slice(g * gd, (g + 1) * gd)
            mixed = _dot(ws_ref[g].astype(BF16), vn[rows, cols]) + bs_ref[:, g:g + 1]
            o_ref[rows, cols] = (u[rows, cols] * mixed).astype(o_ref.dtype)


def _gmlp(y2d, a_ws, a_bs_t, ln_g, ln_b, layer):
    t = y2d.shape[0]
    tl = 512
    return pl.pallas_call(
        _gmlp_kernel,
        grid=(t // tl,),
        in_specs=[
            pl.BlockSpec((tl, A_WIDTH), lambda i: (i, AU0 // A_WIDTH)),
            pl.BlockSpec((tl, A_WIDTH), lambda i: (i, AV0 // A_WIDTH)),
            pl.BlockSpec((None, A_GROUPS, A_CHUNK, A_CHUNK), lambda i: (layer, 0, 0, 0)),
            pl.BlockSpec((None, A_CHUNK, A_GROUPS), lambda i: (layer, 0, 0)),
            pl.BlockSpec((None, 1, A_WIDTH), lambda i: (layer, 0, 0)),
            pl.BlockSpec((None, 1, A_WIDTH), lambda i: (layer, 0, 0)),
        ],
        out_specs=pl.BlockSpec((tl, A_WIDTH), lambda i: (i, 0)),
        out_shape=jax.ShapeDtypeStruct((t, A_WIDTH), BF16),
        compiler_params=_cparams(("parallel",)),
        name="gmlp",
    )(y2d, y2d, a_ws, a_bs_t, ln_g, ln_b)


def _gdn_direction(d, reverse, is_first, is_last, qkv_ref, prev_ref, next_ref, sm_ref,
                   conv_ref, alog_ref, dtb_ref, ext_ref, act_ref, s_ref, o_ref, tl):
    c = DN_CHUNK
    off = HALO - DN_CONV_K // 2
    ext_ref[0:HALO, :] = jnp.where(is_first, 0.0, prev_ref[0].astype(F32))
    ext_ref[HALO:HALO + tl, :] = qkv_ref[0].astype(F32)
    ext_ref[HALO + tl:2 * HALO + tl, :] = jnp.where(is_last, 0.0, next_ref[0].astype(F32))
    for part in range(3):
        cols = slice(part * DN_QK, (part + 1) * DN_QK)
        acc = conv_ref[0:1, cols] * ext_ref[off:off + tl, cols]
        for j in range(1, DN_CONV_K):
            acc = acc + conv_ref[j:j + 1, cols] * ext_ref[off + j:off + j + tl, cols]
        y = jax.nn.silu(acc)
        if part < 2:
            for h in range(DN_HEADS):
                hc = slice(h * DN_DK, (h + 1) * DN_DK)
                yh = y[:, hc]
                yh = yh * lax.rsqrt(jnp.sum(yh * yh, -1, keepdims=True) + EPS)
                act_ref[:, part * DN_QK + h * DN_DK:part * DN_QK + (h + 1) * DN_DK] = yh
        else:
            act_ref[:, cols] = y

    sm = sm_ref[0].astype(F32)
    g_all = -jnp.exp(alog_ref[...]) * jax.nn.softplus(sm + dtb_ref[...])
    beta_all = jax.nn.sigmoid(sm)
    ri = lax.broadcasted_iota(jnp.int32, (tl, tl), 0)
    ci = lax.broadcasted_iota(jnp.int32, (tl, tl), 1)
    same = (ri // c) == (ci // c)
    tri = jnp.where(same & ((ci >= ri) if reverse else (ci <= ri)), 1.0, 0.0).astype(BF16)
    gh, gm, gl = _split3(g_all)
    gam = _dot(tri, gh) + _dot(tri, gm) + _dot(tri, gl)
    gam_t = gam.T

    ii = lax.broadcasted_iota(jnp.int32, (c, c), 0)
    jj = lax.broadcasted_iota(jnp.int32, (c, c), 1)
    incl = (jj >= ii) if reverse else (jj <= ii)
    strict = (jj > ii) if reverse else (jj < ii)
    eye = jnp.where(ii == jj, 1.0, 0.0).astype(F32)
    hi_i, lo_i = (jj, ii) if reverse else (ii, jj)
    pair_masks = []
    s = 1
    while s < c:
        pair_masks.append(((hi_i // (2 * s)) == (lo_i // (2 * s)))
                          & ((hi_i // s) % 2 == 1) & ((lo_i // s) % 2 == 0))
        s *= 2
    scale = DN_DK ** -0.5
    n_ch = tl // c
    order = range(n_ch - 1, -1, -1) if reverse else range(n_ch)
    for n in order:
        rows = slice(n * c, (n + 1) * c)
        for h in range(DN_HEADS):
            lane = d * DN_HEADS + h
            gc = gam[rows, lane:lane + 1]
            gr = gam_t[lane:lane + 1, rows]
            bcol = beta_all[rows, 2 * DN_HEADS + lane:2 * DN_HEADS + lane + 1]
            decay = jnp.where(incl, jnp.exp(jnp.where(incl, gc - gr, 0.0)), 0.0)
            qh = act_ref[rows, h * DN_DK:(h + 1) * DN_DK] * scale
            kh = act_ref[rows, DN_QK + h * DN_DK:DN_QK + (h + 1) * DN_DK]
            vh = act_ref[rows, 2 * DN_QK + h * DN_DV:2 * DN_QK + (h + 1) * DN_DV]
            kb = kh * bcol
            kh16 = kh.astype(BF16)
            low = jnp.where(strict, _dot_nt(kb.astype(BF16), kh16) * decay, 0.0)
            inv = eye - jnp.where(pair_masks[0], low, 0.0)
            for mask in pair_masks[1:]:
                inv16 = inv.astype(BF16)
                inv = inv - _dot(inv16, _dot(jnp.where(mask, low, 0.0).astype(BF16), inv16).astype(BF16))
            eg = jnp.exp(gc)
            rhs = jnp.concatenate([vh * bcol, kb * eg], axis=1).astype(BF16)
            sol = _dot(inv.astype(BF16), rhs)
            u = sol[:, :DN_DV]
            w = sol[:, DN_DV:]
            qk = jnp.where(incl, _dot_nt(qh.astype(BF16), kh16) * decay, 0.0)
            st = s_ref[d, h]
            st16 = st.astype(BF16)
            v_new = u - _dot(w.astype(BF16), st16)
            v16 = v_new.astype(BF16)
            o = _dot((qh * eg).astype(BF16), st16) + _dot(qk.astype(BF16), v16)
            g_last = gc[0:1, :] if reverse else gc[c - 1:c, :]
            k_dec = (kh * jnp.exp(g_last - gc)).astype(BF16)
            s_ref[d, h] = st * jnp.exp(g_last) + _dot_tn(k_dec, v16)
            o_ref[0, rows, h * DN_DV:(h + 1) * DN_DV] = o.astype(o_ref.dtype)


def _gdn_kernel(*refs, tl, n_t, has_init):
    (qkv_f, prev_f, next_f, sm_f, qkv_b, prev_b, next_b, sm_b, conv_ref, alog_ref, dtb_ref) = refs[:11]
    k = 11
    if has_init:
        s0_ref = refs[k]
        k += 1
    o_f, o_b, sfin_ref, ext_ref, act_f, act_b, s_ref = refs[k:]
    i = pl.program_id(1)

    @pl.when(i == 0)
    def _():
        if has_init:
            s_ref[...] = s0_ref[0]
        else:
            s_ref[...] = jnp.zeros(s_ref.shape, F32)

    _gdn_direction(0, False, i == 0, i == n_t - 1, qkv_f, prev_f, next_f, sm_f,
                   conv_ref, alog_ref, dtb_ref, ext_ref, act_f, s_ref, o_f, tl)
    _gdn_direction(1, True, i == n_t - 1, i == 0, qkv_b, prev_b, next_b, sm_b,
                   conv_ref, alog_ref, dtb_ref, ext_ref, act_b, s_ref, o_b, tl)

    @pl.when(i == n_t - 1)
    def _():
        sfin_ref[0] = s_ref[...]


def _gdn(y, conv_w, alog_row, dtb_row, s0, layer):
    b, l, _ = y.shape
    tl = 256
    n_t = l // tl
    hb = tl // HALO
    n_hb = l // HALO
    qkv_blk = QKV0 // QKV_W
    sm_blk = SM0 // LANES
    has_init = s0 is not None

    def fwd(i):
        return i

    def bwd(i):
        return n_t - 1 - i

    def tile_specs(pos):
        return [
            pl.BlockSpec((1, tl, QKV_W), lambda bb, i: (bb, pos(i), qkv_blk)),
            pl.BlockSpec((1, HALO, QKV_W), lambda bb, i: (bb, jnp.maximum(pos(i) * hb - 1, 0), qkv_blk)),
            pl.BlockSpec((1, HALO, QKV_W), lambda bb, i: (bb, jnp.minimum((pos(i) + 1) * hb, n_hb - 1), qkv_blk)),
            pl.BlockSpec((1, tl, LANES), lambda bb, i: (bb, pos(i), sm_blk)),
        ]

    in_specs = tile_specs(fwd) + tile_specs(bwd) + [
        pl.BlockSpec((None, DN_CONV_K, QKV_W), lambda bb, i: (layer, 0, 0)),
        pl.BlockSpec((None, 1, LANES), lambda bb, i: (layer, 0, 0)),
        pl.BlockSpec((None, 1, LANES), lambda bb, i: (layer, 0, 0)),
    ]
    args = [y] * 8 + [conv_w, alog_row, dtb_row]
    if has_init:
        in_specs.append(pl.BlockSpec((1, None, 2, DN_HEADS, DN_DK, DN_DV), lambda bb, i: (bb, layer, 0, 0, 0, 0)))
        args.append(s0)
    return pl.pallas_call(
        functools.partial(_gdn_kernel, tl=tl, n_t=n_t, has_init=has_init),
        grid=(b, n_t),
        in_specs=in_specs,
        out_specs=[
            pl.BlockSpec((1, tl, DN_VW), lambda bb, i: (bb, i, 0)),
            pl.BlockSpec((1, tl, DN_VW), lambda bb, i: (bb, n_t - 1 - i, 0)),
            pl.BlockSpec((1, 2, DN_HEADS, DN_DK, DN_DV), lambda bb, i: (bb, 0, 0, 0, 0)),
        ],
        out_shape=[
            jax.ShapeDtypeStruct((b, l, DN_VW), BF16),
            jax.ShapeDtypeStruct((b, l, DN_VW), BF16),
            jax.ShapeDtypeStruct((b, 2, DN_HEADS, DN_DK, DN_DV), F32),
        ],
        scratch_shapes=[
            pltpu.VMEM((tl + 2 * HALO, QKV_W), F32),
            pltpu.VMEM((tl, QKV_W), F32),
            pltpu.VMEM((tl, QKV_W), F32),
            pltpu.VMEM((2, DN_HEADS, DN_DK, DN_DV), F32),
        ],
        compiler_params=_cparams(("parallel", "arbitrary")),
        name="gdn",
    )(*args)


def _rms(x, g):
    return x * lax.rsqrt(jnp.mean(x * x, -1, keepdims=True) + EPS) * g


def _rope128(x, cos, sin):
    lane = lax.broadcasted_iota(jnp.int32, x.shape, 1)
    first = (lane % (2 * AXIS_FREQS)) < AXIS_FREQS
    sw = jnp.where(first, pltpu.roll(x, LANES - AXIS_FREQS, 1), pltpu.roll(x, AXIS_FREQS, 1))
    return x * cos + sw * sin


def _mla_prep_kernel(*refs, rope, emit_cache):
    cq_ref, ckv_ref, kr_ref, qg_ref, kvg_ref, wqn_ref, wqr_ref, wkn_ref, wv_ref = refs[:9]
    k = 9
    if rope:
        cos_ref, sin_ref = refs[k:k + 2]
        k += 2
    q_out, k_out, v_out = refs[k:k + 3]
    k += 3
    scale = (MLA_NOPE + MLA_ROPE) ** -0.5
    cqn = _rms(cq_ref[0].astype(F32), qg_ref[...]).astype(BF16)
    qn = _dot(cqn, wqn_ref[...]) * scale
    qr = _dot(cqn, wqr_ref[...]) * scale
    ckvn = _rms(ckv_ref[0].astype(F32), kvg_ref[...])
    ckvn16 = ckvn.astype(BF16)
    kn = _dot(ckvn16, wkn_ref[...])
    v_out[0] = _dot(ckvn16, wv_ref[...]).astype(v_out.dtype)
    kr = kr_ref[0].astype(F32)
    if emit_cache:
        ckvn_out, kr_out = refs[k:k + 2]
        ckvn_out[0] = ckvn
        kr_out[0] = kr[:, :MLA_ROPE]
    if rope:
        cos = cos_ref[...]
        sin = sin_ref[...]
        kr = _rope128(kr, cos, sin)
    kr16 = kr.astype(k_out.dtype)
    for h in range(MLA_HEADS):
        qrh = qr[:, h * LANES:(h + 1) * LANES]
        if rope:
            qrh = _rope128(qrh, cos, sin)
        q_out[0, :, h * 256:h * 256 + 128] = qn[:, h * 128:(h + 1) * 128].astype(q_out.dtype)
        q_out[0, :, h * 256 + 128:(h + 1) * 256] = qrh.astype(q_out.dtype)
        k_out[0, :, h * 256:h * 256 + 128] = kn[:, h * 128:(h + 1) * 128].astype(k_out.dtype)
        k_out[0, :, h * 256 + 128:(h + 1) * 256] = kr16


def _mla_prep(y, q_norm, kv_norm, wqn, wqr, wkn, wv, layer, rope_tabs, emit_cache):
    b, l, _ = y.shape
    tm = 256
    rope = rope_tabs is not None
    w_spec = lambda shp: pl.BlockSpec((None,) + shp, lambda bb, i: (layer, 0, 0))
    in_specs = [
        pl.BlockSpec((1, tm, Q_LORA), lambda bb, i: (bb, i, CQ0 // Q_LORA)),
        pl.BlockSpec((1, tm, KV_LORA), lambda bb, i: (bb, i, CKV0 // KV_LORA)),
        pl.BlockSpec((1, tm, LANES), lambda bb, i: (bb, i, KR0 // LANES)),
        w_spec((1, Q_LORA)), w_spec((1, KV_LORA)),
        w_spec((Q_LORA, 512)), w_spec((Q_LORA, 512)), w_spec((KV_LORA, 512)), w_spec((KV_LORA, 512)),
    ]
    args = [y, y, y, q_norm, kv_norm, wqn, wqr, wkn, wv]
    if rope:
        in_specs += [pl.BlockSpec((tm, LANES), lambda bb, i: (i, 0))] * 2
        args += list(rope_tabs)
    out_specs = [
        pl.BlockSpec((1, tm, 1024), lambda bb, i: (bb, i, 0)),
        pl.BlockSpec((1, tm, 1024), lambda bb, i: (bb, i, 0)),
        pl.BlockSpec((1, tm, 512), lambda bb, i: (bb, i, 0)),
    ]
    out_shape = [
        jax.ShapeDtypeStruct((b, l, 1024), BF16),
        jax.ShapeDtypeStruct((b, l, 1024), BF16),
        jax.ShapeDtypeStruct((b, l, 512), BF16),
    ]
    if emit_cache:
        out_specs += [pl.BlockSpec((1, tm, KV_LORA), lambda bb, i: (bb, i, 0)),
                      pl.BlockSpec((1, tm, MLA_ROPE), lambda bb, i: (bb, i, 0))]
        out_shape += [jax.ShapeDtypeStruct((b, l, KV_LORA), F32),
                      jax.ShapeDtypeStruct((b, l, MLA_ROPE), F32)]
    return pl.pallas_call(
        functools.partial(_mla_prep_kernel, rope=rope, emit_cache=emit_cache),
        grid=(b, l // tm),
        in_specs=in_specs,
        out_specs=out_specs,
        out_shape=out_shape,
        compiler_params=_cparams(("parallel", "parallel")),
        name="mla_prep",
    )(*args)


def _mla_ctx_kernel(ckv_ref, kr_ref, wkn_ref, wv_ref, k_out, v_out):
    c16 = ckv_ref[0].astype(BF16)
    kn = _dot(c16, wkn_ref[...])
    v_out[0] = _dot(c16, wv_ref[...]).astype(v_out.dtype)
    kr16 = kr_ref[0].astype(k_out.dtype)
    zeros = jnp.zeros((kr16.shape[0], LANES - MLA_ROPE), k_out.dtype)
    for h in range(MLA_HEADS):
        k_out[0, :, h * 256:h * 256 + 128] = kn[:, h * 128:(h + 1) * 128].astype(k_out.dtype)
        k_out[0, :, h * 256 + 128:h * 256 + 128 + MLA_ROPE] = kr16
        k_out[0, :, h * 256 + 128 + MLA_ROPE:(h + 1) * 256] = zeros


def _mla_ctx(cache_ckv, cache_krope, wkn, wv, layer):
    b, _, p, _ = cache_ckv.shape
    return pl.pallas_call(
        _mla_ctx_kernel,
        grid=(b,),
        in_specs=[
            pl.BlockSpec((1, None, p, KV_LORA), lambda bb: (bb, layer, 0, 0)),
            pl.BlockSpec((1, None, p, MLA_ROPE), lambda bb: (bb, layer, 0, 0)),
            pl.BlockSpec((None, KV_LORA, 512), lambda bb: (layer, 0, 0)),
            pl.BlockSpec((None, KV_LORA, 512), lambda bb: (layer, 0, 0)),
        ],
        out_specs=[pl.BlockSpec((1, p, 1024), lambda bb: (bb, 0, 0)),
                   pl.BlockSpec((1, p, 512), lambda bb: (bb, 0, 0))],
        out_shape=[jax.ShapeDtypeStruct((b, p, 1024), BF16),
                   jax.ShapeDtypeStruct((b, p, 512), BF16)],
        compiler_params=_cparams(("parallel",)),
        name="mla_ctx",
    )(cache_ckv, cache_krope, wkn, wv)


def _attn_kernel(*refs, has_ctx, nk):
    q_ref = refs[0]
    if has_ctx:
        kc_ref, vc_ref, k_ref, v_ref, o_ref, m_ref, l_ref, acc_ref = refs[1:]
    else:
        k_ref, v_ref, o_ref, m_ref, l_ref, acc_ref = refs[1:]
    kk = pl.program_id(2)

    @pl.when(kk == 0)
    def _():
        m_ref[...] = jnp.full(m_ref.shape, -jnp.inf, F32)
        l_ref[...] = jnp.zeros(l_ref.shape, F32)
        acc_ref[...] = jnp.zeros(acc_ref.shape, F32)

    def body(kr, vr):
        for h in range(MLA_HEADS):
            s = _dot_nt(q_ref[0, :, h * 256:(h + 1) * 256], kr[0, :, h * 256:(h + 1) * 256])
            m_prev = m_ref[h]
            m_new = jnp.maximum(m_prev, jnp.max(s, -1, keepdims=True))
            alpha = jnp.exp(m_prev - m_new)
            p = jnp.exp(s - m_new)
            l_ref[h] = alpha * l_ref[h] + jnp.sum(p, -1, keepdims=True)
            hc = slice(h * MLA_V, (h + 1) * MLA_V)
            acc_ref[:, hc] = alpha * acc_ref[:, hc] + _dot(p.astype(BF16), vr[0, :, hc])
            m_ref[h] = m_new

    if has_ctx:
        @pl.when(kk == 0)
        def _():
            body(kc_ref, vc_ref)

        @pl.when(kk > 0)
        def _():
            body(k_ref, v_ref)
    else:
        body(k_ref, v_ref)

    @pl.when(kk == nk - 1)
    def _():
        for h in range(MLA_HEADS):
            hc = slice(h * MLA_V, (h + 1) * MLA_V)
            o_ref[0, :, hc] = (acc_ref[:, hc] / l_ref[h]).astype(o_ref.dtype)


def _attention(q, k, v, ctx_kv, tq, tk):
    b, l, _ = q.shape
    has_ctx = ctx_kv is not None
    n_lat = l // tk
    nk = n_lat + (1 if has_ctx else 0)
    in_specs = [pl.BlockSpec((1, tq, 1024), lambda bb, i, j: (bb, i, 0))]
    args = [q]
    if has_ctx:
        assert ctx_kv[0].shape[1] == tk
        in_specs += [pl.BlockSpec((1, tk, 1024), lambda bb, i, j: (bb, 0, 0)),
                     pl.BlockSpec((1, tk, 512), lambda bb, i, j: (bb, 0, 0))]
        args += list(ctx_kv)
        kv_idx = lambda bb, i, j: (bb, jnp.maximum(j - 1, 0), 0)
    else:
        kv_idx = lambda bb, i, j: (bb, j, 0)
    in_specs += [pl.BlockSpec((1, tk, 1024), kv_idx), pl.BlockSpec((1, tk, 512), kv_idx)]
    args += [k, v]
    return pl.pallas_call(
        functools.partial(_attn_kernel, has_ctx=has_ctx, nk=nk),
        grid=(b, l // tq, nk),
        in_specs=in_specs,
        out_specs=pl.BlockSpec((1, tq, 512), lambda bb, i, j: (bb, i, 0)),
        out_shape=jax.ShapeDtypeStruct((b, l, 512), BF16),
        scratch_shapes=[
            pltpu.VMEM((MLA_HEADS, tq, 1), F32),
            pltpu.VMEM((MLA_HEADS, tq, 1), F32),
            pltpu.VMEM((tq, MLA_HEADS * MLA_V), F32),
        ],
        compiler_params=_cparams(("parallel", "parallel", "arbitrary")),
        name="attn",
    )(*args)


def _merge_kernel(oa_ref, of_ref, ob_ref, dz_ref, oc_ref, gate_ref, x_ref, mod_ref, dng_ref,
                  wpa_ref, wpb_ref, wpc_ref, wout_ref, lng_ref, lnb_ref, o_ref):
    dng = dng_ref[...]
    s = of_ref[0].astype(F32) + ob_ref[0].astype(F32)
    dz = dz_ref[0].astype(F32)
    parts = []
    for h in range(DN_HEADS):
        hc = slice(h * DN_DV, (h + 1) * DN_DV)
        parts.append((_rms(s[:, hc], dng) * jax.nn.silu(dz[:, hc])).astype(BF16))
    o_dn = jnp.concatenate(parts, axis=1)
    gates = gate_ref[0]
    ga = jax.nn.sigmoid(gates[:, 0:D_MODEL].astype(F32))
    merged = ga * _dot(oa_ref[0], wpa_ref[...])
    gb = jax.nn.sigmoid(gates[:, D_MODEL:2 * D_MODEL].astype(F32))
    merged = merged + gb * _dot(o_dn, wpb_ref[...])
    gc = jax.nn.sigmoid(gates[:, 2 * D_MODEL:3 * D_MODEL].astype(F32))
    merged = merged + gc * _dot(oc_ref[0], wpc_ref[...])
    mix = _dot(merged.astype(BF16), wout_ref[...])
    g1 = mod_ref[0, 2:3, :]
    r = DN_ALPHA * x_ref[0] + g1 * mix
    o_ref[0] = _layer_norm(r, lng_ref[...], lnb_ref[...])


def _merge(o_a, o_f, o_b, y, o_c, x, mod, dn_norm, w_pa, w_pb, w_pc, w_out, ln_g, ln_b, layer):
    bm, lm, _ = x.shape
    tm = 512
    tok = lambda w, blk: pl.BlockSpec((1, tm, w), lambda b, i: (b, i, blk))
    wsp = lambda shp: pl.BlockSpec((None,) + shp, lambda b, i: (layer, 0, 0))
    return pl.pallas_call(
        _merge_kernel,
        grid=(bm, lm // tm),
        in_specs=[
            tok(512, 0), tok(512, 0), tok(512, 0), tok(512, DZ0 // 512), tok(512, 0),
            tok(3 * D_MODEL, 0), tok(D_MODEL, 0),
            pl.BlockSpec((1, 6, D_MODEL), lambda b, i: (b, 0, 0)),
            wsp((1, DN_DV)),
            wsp((A_WIDTH, D_MODEL)), wsp((DN_VW, D_MODEL)), wsp((512, D_MODEL)), wsp((D_MODEL, D_MODEL)),
            pl.BlockSpec((None, None, 1, D_MODEL), lambda b, i: (layer, 0, 0, 0)),
            pl.BlockSpec((None, None, 1, D_MODEL), lambda b, i: (layer, 0, 0, 0)),
        ],
        out_specs=tok(D_MODEL, 0),
        out_shape=jax.ShapeDtypeStruct((bm, lm, D_MODEL), F32),
        compiler_params=_cparams(("parallel", "parallel")),
        name="merge",
    )(o_a, o_f, o_b, y, o_c, y, x, mod, dn_norm, w_pa, w_pb, w_pc, w_out, ln_g, ln_b)


def _ffn_kernel(x_ref, mod_ref, wg_ref, wu_ref, wd_ref, lng_ref, lnb_ref, o_ref, h_ref, acc_ref, *, nf):
    f = pl.program_id(2)

    @pl.when(f == 0)
    def _():
        h_ref[...] = (x_ref[0] * (1.0 + mod_ref[0, 4:5, :]) + mod_ref[0, 3:4, :]).astype(BF16)
        acc_ref[...] = jnp.zeros(acc_ref.shape, F32)

    h = h_ref[...]
    act = (jax.nn.silu(_dot(h, wg_ref[...])) * _dot(h, wu_ref[...])).astype(BF16)
    acc_ref[...] += _dot(act, wd_ref[...])

    @pl.when(f == nf - 1)
    def _():
        r = DN_ALPHA * x_ref[0] + mod_ref[0, 5:6, :] * acc_ref[...]
        o_ref[0] = _layer_norm(r, lng_ref[...], lnb_ref[...])


def _ffn(x, mod, w_gu, w_down, ln_g, ln_b, layer, idx):
    bm, lm, _ = x.shape
    tm, tf = 1024, 256
    nf = D_FF // tf
    return pl.pallas_call(
        functools.partial(_ffn_kernel, nf=nf),
        grid=(bm, lm // tm, nf),
        in_specs=[
            pl.BlockSpec((1, tm, D_MODEL), lambda b, i, f: (b, i, 0)),
            pl.BlockSpec((1, 6, D_MODEL), lambda b, i, f: (b, 0, 0)),
            pl.BlockSpec((None, D_MODEL, tf), lambda b, i, f: (idx, 0, f)),
            pl.BlockSpec((None, D_MODEL, tf), lambda b, i, f: (idx, 0, nf + f)),
            pl.BlockSpec((None, tf, D_MODEL), lambda b, i, f: (idx, f, 0)),
            pl.BlockSpec((None, None, 1, D_MODEL), lambda b, i, f: (layer, 1, 0, 0)),
            pl.BlockSpec((None, None, 1, D_MODEL), lambda b, i, f: (layer, 1, 0, 0)),
        ],
        out_specs=pl.BlockSpec((1, tm, D_MODEL), lambda b, i, f: (b, i, 0)),
        out_shape=jax.ShapeDtypeStruct((bm, lm, D_MODEL), F32),
        scratch_shapes=[pltpu.VMEM((tm, D_MODEL), BF16), pltpu.VMEM((tm, D_MODEL), F32)],
        compiler_params=_cparams(("parallel", "parallel", "arbitrary")),
        name="ffn",
    )(x, mod, w_gu, w_gu, w_down, ln_g, ln_b)


def _route(h, wr, br):
    logits = _dot_f32(h, wr)
    lane = lax.broadcasted_iota(jnp.int32, logits.shape, 1)
    valid = lane < N_EXPERTS
    neg = -jnp.inf
    sel = jnp.where(valid, logits + br, neg)
    m1 = jnp.max(sel, -1, keepdims=True)
    i1 = jnp.min(jnp.where(sel == m1, lane, LANES), -1, keepdims=True)
    sel2 = jnp.where(lane == i1, neg, sel)
    m2 = jnp.max(sel2, -1, keepdims=True)
    i2 = jnp.min(jnp.where(sel2 == m2, lane, LANES), -1, keepdims=True)
    l1 = jnp.sum(jnp.where(lane == i1, logits, 0.0), -1, keepdims=True)
    l2 = jnp.sum(jnp.where(lane == i2, logits, 0.0), -1, keepdims=True)
    mx = jnp.maximum(l1, l2)
    e1 = jnp.exp(l1 - mx)
    e2 = jnp.exp(l2 - mx)
    den = e1 + e2
    return jnp.where(lane == i1, e1 / den, 0.0) + jnp.where(lane == i2, e2 / den, 0.0)


def _moe_kernel(x_ref, mod_ref, wr_ref, br_ref, wg_ref, wu_ref, wd_ref, lng_ref, lnb_ref, o_ref,
                h_ref, comb_ref, acc_ref):
    e = pl.program_id(2)

    @pl.when(e == 0)
    def _():
        hf = x_ref[0] * (1.0 + mod_ref[0, 4:5, :]) + mod_ref[0, 3:4, :]
        h_ref[...] = hf.astype(BF16)
        comb_ref[...] = _route(hf, wr_ref[...], br_ref[...])
        acc_ref[...] = jnp.zeros(acc_ref.shape, F32)

    h = h_ref[...]
    act = (jax.nn.silu(_dot(h, wg_ref[...])) * _dot(h, wu_ref[...])).astype(BF16)
    lane = lax.broadcasted_iota(jnp.int32, comb_ref.shape, 1)
    ce = jnp.sum(jnp.where(lane == e, comb_ref[...], 0.0), -1, keepdims=True)
    acc_ref[...] += ce * _dot(act, wd_ref[...])

    @pl.when(e == N_EXPERTS - 1)
    def _():
        r = DN_ALPHA * x_ref[0] + mod_ref[0, 5:6, :] * acc_ref[...]
        o_ref[0] = _layer_norm(r, lng_ref[...], lnb_ref[...])


def _moe(x, mod, w_router_p, b_router_p, w_gu, w_down, ln_g, ln_b, layer, idx):
    bm, lm, _ = x.shape
    tm = 512
    return pl.pallas_call(
        _moe_kernel,
        grid=(bm, lm // tm, N_EXPERTS),
        in_specs=[
            pl.BlockSpec((1, tm, D_MODEL), lambda b, i, e: (b, i, 0)),
            pl.BlockSpec((1, 6, D_MODEL), lambda b, i, e: (b, 0, 0)),
            pl.BlockSpec((None, D_MODEL, LANES), lambda b, i, e: (idx, 0, 0)),
            pl.BlockSpec((None, 1, LANES), lambda b, i, e: (idx, 0, 0)),
            pl.BlockSpec((None, None, D_MODEL, D_FF_EXPERT), lambda b, i, e: (idx, e, 0, 0)),
            pl.BlockSpec((None, None, D_MODEL, D_FF_EXPERT), lambda b, i, e: (idx, e, 0, 1)),
            pl.BlockSpec((None, None, D_FF_EXPERT, D_MODEL), lambda b, i, e: (idx, e, 0, 0)),
            pl.BlockSpec((None, None, 1, D_MODEL), lambda b, i, e: (layer, 1, 0, 0)),
            pl.BlockSpec((None, None, 1, D_MODEL), lambda b, i, e: (layer, 1, 0, 0)),
        ],
        out_specs=pl.BlockSpec((1, tm, D_MODEL), lambda b, i, e: (b, i, 0)),
        out_shape=jax.ShapeDtypeStruct((bm, lm, D_MODEL), F32),
        scratch_shapes=[pltpu.VMEM((tm, D_MODEL), BF16), pltpu.VMEM((tm, LANES), F32),
                        pltpu.VMEM((tm, D_MODEL), F32)],
        compiler_params=_cparams(("parallel", "parallel", "arbitrary")),
        name="moe",
    )(x, mod, w_router_p, b_router_p, w_gu, w_gu, w_down, ln_g, ln_b)


def _pack_w_in(w_in):
    d = w_in.shape[0]
    o = 0
    cols = {}
    for name, width in (("a_u", A_WIDTH), ("a_v", A_WIDTH), ("dq", DN_QK), ("dk", DN_QK), ("dv", DN_VW),
                        ("dz", DN_VW), ("sm", 4 * DN_HEADS), ("cq", Q_LORA), ("ckv", KV_LORA), ("kr", MLA_ROPE),
                        ("ga", D_MODEL), ("gb", D_MODEL), ("gc", D_MODEL)):
        cols[name] = w_in[:, :, o:o + width]
        o += width
    zpad = lambda n: jnp.zeros((d, D_MODEL, n), w_in.dtype)
    packed = jnp.concatenate(
        [cols["ga"], cols["gb"], cols["gc"], cols["dq"], cols["dk"], cols["dv"], cols["dz"],
         cols["a_u"], cols["a_v"], cols["cq"], cols["ckv"],
         cols["kr"], zpad(LANES - MLA_ROPE), cols["sm"], zpad(LANES - 4 * DN_HEADS)], axis=-1)
    return packed.astype(BF16)


def _rope_tables(n_tok):
    row = (jnp.arange(n_tok) // GRID_W).astype(F32)
    col = (jnp.arange(n_tok) % GRID_W).astype(F32)
    inv_freq = ROPE_BASE ** (-jnp.arange(AXIS_FREQS, dtype=F32) / AXIS_FREQS)
    ang_r = row[:, None] * inv_freq
    ang_c = col[:, None] * inv_freq
    ones = jnp.ones((n_tok, LANES - MLA_ROPE), F32)
    cos = jnp.concatenate([jnp.cos(ang_r), jnp.cos(ang_r), jnp.cos(ang_c), jnp.cos(ang_c), ones], axis=1)
    sin = jnp.concatenate([-jnp.sin(ang_r), jnp.sin(ang_r), -jnp.sin(ang_c), jnp.sin(ang_c), 0.0 * ones], axis=1)
    return cos, sin


def kernel(x_prompt, x_sample, state_dn, cache_ckv, cache_krope, c, c_ctx, w_mod, b_mod, w_in, a_ln_g, a_ln_b, a_ws, a_bs, dn_conv, dn_a_log, dn_dt_bias, dn_norm, q_norm, w_qb, kv_norm, w_kvb, w_pa, w_pb, w_pc, w_out, ln_g, ln_b, ffn_gu, ffn_down, moe_router, moe_bias, moe_gu, moe_down):
    batch, seq, _ = x_prompt.shape
    dec_batch, dec_seq, _ = x_sample.shape
    depth = w_in.shape[0]
    assert depth == DEPTH

    w_in_p = _pack_w_in(w_in)
    a_bs_t = jnp.swapaxes(a_bs, 1, 2)
    ln_a_g = a_ln_g.reshape(depth, 1, A_WIDTH)
    ln_a_b = a_ln_b.reshape(depth, 1, A_WIDTH)
    lane_pad = lambda v: jnp.pad(v.reshape(depth, 1, -1), ((0, 0), (0, 0), (0, LANES - v[0].size)))
    alog_row = lane_pad(dn_a_log)
    dtb_row = lane_pad(dn_dt_bias)
    dn_g = dn_norm.reshape(depth, 1, DN_DV)
    qg = q_norm.reshape(depth, 1, Q_LORA)
    kvg = kv_norm.reshape(depth, 1, KV_LORA)
    wq = w_qb.reshape(depth, Q_LORA, MLA_HEADS, MLA_NOPE + MLA_ROPE)
    wqn = wq[..., :MLA_NOPE].reshape(depth, Q_LORA, MLA_HEADS * MLA_NOPE).astype(BF16)
    wqr = jnp.pad(wq[..., MLA_NOPE:], ((0, 0), (0, 0), (0, 0), (0, LANES - MLA_ROPE)))
    wqr = wqr.reshape(depth, Q_LORA, MLA_HEADS * LANES).astype(BF16)
    wkv = w_kvb.reshape(depth, KV_LORA, MLA_HEADS, MLA_NOPE + MLA_V)
    wkn = wkv[..., :MLA_NOPE].reshape(depth, KV_LORA, MLA_HEADS * MLA_NOPE).astype(BF16)
    wv = wkv[..., MLA_NOPE:].reshape(depth, KV_LORA, MLA_HEADS * MLA_V).astype(BF16)
    w_pa16, w_pb16, w_pc16, w_out16 = (w.astype(BF16) for w in (w_pa, w_pb, w_pc, w_out))
    ffn_gu16, ffn_down16 = ffn_gu.astype(BF16), ffn_down.astype(BF16)
    moe_gu16, moe_down16 = moe_gu.astype(BF16), moe_down.astype(BF16)
    n_moe = moe_router.shape[0]
    wr_p = jnp.pad(moe_router, ((0, 0), (0, 0), (0, LANES - N_EXPERTS)))
    br_p = jnp.pad(moe_bias.reshape(n_moe, 1, N_EXPERTS), ((0, 0), (0, 0), (0, LANES - N_EXPERTS)))
    ln_g4 = ln_g.reshape(depth, 2, 1, D_MODEL)
    ln_b4 = ln_b.reshape(depth, 2, 1, D_MODEL)
    rope_tabs = _rope_tables(dec_seq)

    n_cond = 16
    cond = jnp.concatenate([c, c_ctx[None, :], jnp.zeros((n_cond - dec_batch - 1, D_MODEL), F32)], axis=0)
    mods = _adaln(cond, w_mod, b_mod).reshape(depth, n_cond, 6, D_MODEL)

    xp = x_prompt.reshape(1, batch * seq, D_MODEL)
    xs = x_sample
    st_dn, st_ckv, st_kr = [], [], []

    def channel_mixer(x, mod, l):
        if l % 2 == 0:
            return _ffn(x, mod, ffn_gu16, ffn_down16, ln_g4, ln_b4, l, l // 2)
        return _moe(x, mod, wr_p, br_p, moe_gu16, moe_down16, ln_g4, ln_b4, l, l // 2)

    def token_mixer(x, mod, l, bsz, n_tok, latent):
        y = _inproj(x, mod, w_in_p, l)
        o_a = _gmlp(y.reshape(-1, NP), a_ws, a_bs_t, ln_a_g, ln_a_b, l).reshape(x.shape[0], x.shape[1], A_WIDTH)
        ys = y.reshape(bsz, n_tok, NP)
        o_f, o_b, s_fin = _gdn(ys, dn_conv, alog_row, dtb_row, state_dn if latent else None, l)
        if latent:
            q, k, v = _mla_prep(ys, qg, kvg, wqn, wqr, wkn, wv, l, rope_tabs, False)
            ctx_kv = _mla_ctx(cache_ckv, cache_krope, wkn, wv, l)
            o_c = _attention(q, k, v, ctx_kv, 512, 512)
            extra = None
        else:
            q, k, v, ckv_n, kr = _mla_prep(ys, qg, kvg, wqn, wqr, wkn, wv, l, None, True)
            o_c = _attention(q, k, v, None, n_tok, n_tok)
            extra = (s_fin, ckv_n, kr)
        shp = (x.shape[0], x.shape[1], 512)
        x = _merge(o_a, o_f.reshape(shp), o_b.reshape(shp), y, o_c.reshape(shp), x, mod, dn_g,
                   w_pa16, w_pb16, w_pc16, w_out16, ln_g4, ln_b4, l)
        return x, extra

    for l in range(depth):
        mod_p = mods[l, dec_batch:dec_batch + 1]
        mod_s = mods[l, :dec_batch]
        xp, (s_fin, ckv_n, kr) = token_mixer(xp, mod_p, l, batch, seq, False)
        xp = channel_mixer(xp, mod_p, l)
        st_dn.append(s_fin)
        st_ckv.append(ckv_n)
        st_kr.append(kr)
        xs, _ = token_mixer(xs, mod_s, l, dec_batch, dec_seq, True)
        xs = channel_mixer(xs, mod_s, l)

    return (xp.reshape(batch, seq, D_MODEL), xs,
            jnp.stack(st_dn, axis=1), jnp.stack(st_ckv, axis=1), jnp.stack(st_kr, axis=1))
```

```python
import functools
import math

import jax
import jax.numpy as jnp
from jax import lax
from jax.experimental import pallas as pl
from jax.experimental.pallas import tpu as pltpu

F32 = jnp.float32
BF16 = jnp.bfloat16

D_MODEL = 1024
DEPTH = 4
GRID_W = 64
A_WIDTH = 512
A_CHUNK = 128
A_GROUPS = 4
DN_HEADS = 4
DN_DK = 128
DN_DV = 128
DN_QK = DN_HEADS * DN_DK
DN_VW = DN_HEADS * DN_DV
DN_CONV_K = 5
GDN_TILE = 256
MLA_HEADS = 4
MLA_NOPE = 128
MLA_ROPE = 64
MLA_V = 128
Q_LORA = 256
KV_LORA = 256
AXIS_FREQS = MLA_ROPE // 4
ROPE_BASE = 10000.0
D_FF = 2816
N_EXPERTS = 8
D_FF_EXPERT = D_FF // 2
DN_ALPHA = (2 * DEPTH) ** 0.25
EPS = 1e-6
LN_EPS = 1e-5

LANES = 128
HALO = 16
VMEM_LIMIT = 56 * 1024 * 1024

GATE0 = 0
QKV0 = 3 * D_MODEL
DZ0 = QKV0 + 3 * DN_QK
AU0 = DZ0 + DN_VW
AV0 = AU0 + A_WIDTH
CQ0 = AV0 + A_WIDTH
CKV0 = CQ0 + Q_LORA
KR0 = CKV0 + KV_LORA
SM0 = KR0 + LANES
NP = SM0 + LANES
QKV_W = 3 * DN_QK


def _cparams(sem):
    return pltpu.CompilerParams(dimension_semantics=sem, vmem_limit_bytes=VMEM_LIMIT)


def _dot(a, b):
    return jnp.dot(a, b, preferred_element_type=F32)


def _dot_nt(a, b):
    return lax.dot_general(a, b, (((1,), (1,)), ((), ())), preferred_element_type=F32)


def _dot_tn(a, b):
    return lax.dot_general(a, b, (((0,), (0,)), ((), ())), preferred_element_type=F32)


def _split3(x):
    hi = x.astype(BF16)
    r = x - hi.astype(F32)
    mid = r.astype(BF16)
    lo = (r - mid.astype(F32)).astype(BF16)
    return hi, mid, lo


def _dot_f32(a, b):
    ah, am, al = _split3(a)
    bh, bm, bl = _split3(b)
    return (_dot(ah, bh) + (_dot(ah, bm) + _dot(am, bh))
            + (_dot(am, bm) + _dot(ah, bl) + _dot(al, bh)))


def _layer_norm(r, g, b):
    mu = jnp.mean(r, -1, keepdims=True)
    rc = r - mu
    var = jnp.mean(rc * rc, -1, keepdims=True)
    return rc * lax.rsqrt(var + LN_EPS) * g + b


def _adaln_kernel(c_ref, w_ref, b_ref, o_ref):
    s = jax.nn.silu(c_ref[...])
    o_ref[...] = _dot_f32(s, w_ref[...]) + b_ref[...]


def _adaln(cond, w_mod, b_mod):
    rows = cond.shape[0]
    n_out = w_mod.shape[-1]
    tn = 1536
    return pl.pallas_call(
        _adaln_kernel,
        grid=(DEPTH, n_out // tn),
        in_specs=[
            pl.BlockSpec((rows, D_MODEL), lambda l, j: (0, 0)),
            pl.BlockSpec((None, D_MODEL, tn), lambda l, j: (l, 0, j)),
            pl.BlockSpec((None, 1, tn), lambda l, j: (l, 0, j)),
        ],
        out_specs=pl.BlockSpec((None, rows, tn), lambda l, j: (l, 0, j)),
        out_shape=jax.ShapeDtypeStruct((DEPTH, rows, n_out), F32),
        compiler_params=_cparams(("parallel", "parallel")),
        name="adaln",
    )(cond, w_mod, b_mod.reshape(DEPTH, 1, n_out))


def _inproj_kernel(x_ref, mod_ref, w_ref, y_ref, h_ref):
    @pl.when(pl.program_id(2) == 0)
    def _():
        sh = mod_ref[0, 0:1, :]
        sc = mod_ref[0, 1:2, :]
        h_ref[...] = (x_ref[0] * (1.0 + sc) + sh).astype(BF16)

    y_ref[0] = _dot(h_ref[...], w_ref[...]).astype(y_ref.dtype)


def _inproj(x, mod, w_in_p, layer):
    bm, lm, _ = x.shape
    tm, tn = 1024, 1152
    return pl.pallas_call(
        _inproj_kernel,
        grid=(bm, lm // tm, NP // tn),
        in_specs=[
            pl.BlockSpec((1, tm, D_MODEL), lambda b, i, j: (b, i, 0)),
            pl.BlockSpec((1, 6, D_MODEL), lambda b, i, j: (b, 0, 0)),
            pl.BlockSpec((None, D_MODEL, tn), lambda b, i, j: (layer, 0, j)),
        ],
        out_specs=pl.BlockSpec((1, tm, tn), lambda b, i, j: (b, i, j)),
        out_shape=jax.ShapeDtypeStruct((bm, lm, NP), BF16),
        scratch_shapes=[pltpu.VMEM((tm, D_MODEL), BF16)],
        compiler_params=_cparams(("parallel", "parallel", "arbitrary")),
        name="inproj",
    )(x, mod, w_in_p)


def _gmlp_kernel(u_ref, v_ref, ws_ref, bs_ref, g_ref, b_ref, o_ref):
    tl = u_ref.shape[0]
    gd = A_WIDTH // A_GROUPS
    v = jax.nn.gelu(v_ref[...].astype(F32))
    vn = _layer_norm(v, g_ref[...], b_ref[...]).astype(BF16)
    u = jax.nn.gelu(u_ref[...].astype(F32))
    for n in range(tl // A_CHUNK):
        rows = slice(n * A_CHUNK, (n + 1) * A_CHUNK)
        for g in range(A_GROUPS):
            cols = slice(g * gd, (g + 1) * gd)
            mixed = _dot(ws_ref[g].astype(BF16), vn[rows, cols]) + bs_ref[:, g:g + 1]
            o_ref[rows, cols] = (u[rows, cols] * mixed).astype(o_ref.dtype)


def _gmlp(y2d, a_ws, a_bs_t, ln_g, ln_b, layer):
    t = y2d.shape[0]
    tl = 512
    return pl.pallas_call(
        _gmlp_kernel,
        grid=(t // tl,),
        in_specs=[
            pl.BlockSpec((tl, A_WIDTH), lambda i: (i, AU0 // A_WIDTH)),
            pl.BlockSpec((tl, A_WIDTH), lambda i: (i, AV0 // A_WIDTH)),
            pl.BlockSpec((None, A_GROUPS, A_CHUNK, A_CHUNK), lambda i: (layer, 0, 0, 0)),
            pl.BlockSpec((None, A_CHUNK, A_GROUPS), lambda i: (layer, 0, 0)),
            pl.BlockSpec((None, 1, A_WIDTH), lambda i: (layer, 0, 0)),
            pl.BlockSpec((None, 1, A_WIDTH), lambda i: (layer, 0, 0)),
        ],
        out_specs=pl.BlockSpec((tl, A_WIDTH), lambda i: (i, 0)),
        out_shape=jax.ShapeDtypeStruct((t, A_WIDTH), BF16),
        compiler_params=_cparams(("parallel",)),
        name="gmlp",
    )(y2d, y2d, a_ws, a_bs_t, ln_g, ln_b)


def _gdn_prep(is_first, is_last, qkv_ref, prev_ref, next_ref, sm_ref,
              conv_ref, alog_ref, dtb_ref, tri_ref, ext_ref, act_ref, tl):
    off = HALO - DN_CONV_K // 2
    ext_ref[0:HALO, :] = jnp.where(is_first, 0.0, prev_ref[0].astype(F32))
    ext_ref[HALO:HALO + tl, :] = qkv_ref[0].astype(F32)
    ext_ref[HALO + tl:2 * HALO + tl, :] = jnp.where(is_last, 0.0, next_ref[0].astype(F32))
    for part in range(3):
        cols = slice(part * DN_QK, (part + 1) * DN_QK)
        acc = conv_ref[0:1, cols] * ext_ref[off:off + tl, cols]
        for j in range(1, DN_CONV_K):
            acc = acc + conv_ref[j:j + 1, cols] * ext_ref[off + j:off + j + tl, cols]
        y = jax.nn.silu(acc)
        if part < 2:
            for h in range(DN_HEADS):
                hc = slice(h * DN_DK, (h + 1) * DN_DK)
                yh = y[:, hc]
                yh = yh * lax.rsqrt(jnp.sum(yh * yh, -1, keepdims=True) + EPS)
                act_ref[:, part * DN_QK + h * DN_DK:part * DN_QK + (h + 1) * DN_DK] = yh
        else:
            act_ref[:, cols] = y

    sm = sm_ref[0].astype(F32)
    g_all = -jnp.exp(alog_ref[...]) * jax.nn.softplus(sm + dtb_ref[...])
    beta_all = jax.nn.sigmoid(sm)
    tri = tri_ref[...]
    gh, gm, gl = _split3(g_all)
    gam = _dot(tri, gh) + _dot(tri, gm) + _dot(tri, gl)
    return gam, gam.T, beta_all


def _gdn_kernel(*refs, tl, n_t, has_init):
    (qkv_f, prev_f, next_f, sm_f, qkv_b, prev_b, next_b, sm_b,
     conv_ref, alog_ref, dtb_ref, tri_ref, lvl_ref) = refs[:13]
    k = 13
    if has_init:
        s0_ref = refs[k]
        k += 1
    o_f, o_b, sfin_ref, ext_f, ext_b, act_f, act_b, s_ref = refs[k:]
    i = pl.program_id(1)
    c = tl
    n_lvl = lvl_ref.shape[1]

    @pl.when(i == 0)
    def _():
        if has_init:
            s_ref[...] = s0_ref[0]
        else:
            s_ref[...] = jnp.zeros(s_ref.shape, F32)

    gam_f = _gdn_prep(i == 0, i == n_t - 1, qkv_f, prev_f, next_f, sm_f,
                      conv_ref, alog_ref, dtb_ref, tri_ref.at[0], ext_f, act_f, tl)
    gam_b = _gdn_prep(i == n_t - 1, i == 0, qkv_b, prev_b, next_b, sm_b,
                      conv_ref, alog_ref, dtb_ref, tri_ref.at[1], ext_b, act_b, tl)

    ii = lax.broadcasted_iota(jnp.int32, (c, c), 0)
    jj = lax.broadcasted_iota(jnp.int32, (c, c), 1)
    eye = jnp.where(ii == jj, 1.0, 0.0).astype(F32)
    scale = DN_DK ** -0.5
    dirs = ((0, False, gam_f, act_f, o_f), (1, True, gam_b, act_b, o_b))

    probs = []
    for d, reverse, (gam, gam_t, beta_all), act_ref, o_ref in dirs:
        incl = (jj >= ii) if reverse else (jj <= ii)
        strict = (jj > ii) if reverse else (jj < ii)
        for h in range(DN_HEADS):
            lane = d * DN_HEADS + h
            gc = gam[:, lane:lane + 1]
            gr = gam_t[lane:lane + 1, :]
            bcol = beta_all[:, 2 * DN_HEADS + lane:2 * DN_HEADS + lane + 1]
            decay = jnp.where(incl, jnp.exp(jnp.where(incl, gc - gr, 0.0)), 0.0)
            qh = act_ref[:, h * DN_DK:(h + 1) * DN_DK] * scale
            kh = act_ref[:, DN_QK + h * DN_DK:DN_QK + (h + 1) * DN_DK]
            vh = act_ref[:, 2 * DN_QK + h * DN_DV:2 * DN_QK + (h + 1) * DN_DV]
            kb = kh * bcol
            kh16 = kh.astype(BF16)
            low16 = jnp.where(strict, _dot_nt(kb.astype(BF16), kh16) * decay, 0.0).astype(BF16)
            qk16 = jnp.where(incl, _dot_nt(qh.astype(BF16), kh16) * decay, 0.0).astype(BF16)
            eg = jnp.exp(gc)
            g_last = gc[0:1, :] if reverse else gc[c - 1:c, :]
            probs.append(dict(
                d=d, h=h, o_ref=o_ref, low16=low16, qk16=qk16,
                rhs16=jnp.concatenate([vh * bcol, kb * eg], axis=1).astype(BF16),
                qe16=(qh * eg).astype(BF16),
                kdec16=(kh * jnp.exp(g_last - gc)).astype(BF16),
                e_last=jnp.exp(g_last),
                inv=eye - (low16 * lvl_ref[d, 0]).astype(F32)))

    for lv in range(1, n_lvl):
        for p in probs:
            p["inv16"] = p["inv"].astype(BF16)
            p["t16"] = _dot(p["low16"] * lvl_ref[p["d"], lv], p["inv16"]).astype(BF16)
        for p in probs:
            p["inv"] = p["inv"] - _dot(p["inv16"], p["t16"])

    for p in probs:
        sol = _dot(p["inv"].astype(BF16), p["rhs16"])
        p["u"] = sol[:, :DN_DV]
        p["w16"] = sol[:, DN_DV:].astype(BF16)
    for p in probs:
        st = s_ref[p["d"], p["h"]]
        st16 = st.astype(BF16)
        v16 = (p["u"] - _dot(p["w16"], st16)).astype(BF16)
        o = _dot(p["qe16"], st16) + _dot(p["qk16"], v16)
        s_ref[p["d"], p["h"]] = st * p["e_last"] + _dot_tn(p["kdec16"], v16)
        p["o_ref"][0, :, p["h"] * DN_DV:(p["h"] + 1) * DN_DV] = o.astype(p["o_ref"].dtype)

    @pl.when(i == n_t - 1)
    def _():
        sfin_ref[0] = s_ref[...]


def _gdn_tables(c):
    ii = jnp.arange(c)[:, None]
    jj = jnp.arange(c)[None, :]
    tri = jnp.stack([jj <= ii, jj >= ii]).astype(BF16)
    x = ii ^ jj
    lvls = []
    s = 1
    while s < c:
        pair = (x >= s) & (x < 2 * s)
        lvls.append(jnp.stack([pair & (jj < ii), pair & (jj > ii)]))
        s *= 2
    return tri, jnp.stack(lvls, axis=1).astype(BF16)


def _gdn(y, conv_w, alog_row, dtb_row, tables, s0, layer):
    b, l, _ = y.shape
    tl = GDN_TILE
    n_t = l // tl
    hb = tl // HALO
    n_hb = l // HALO
    qkv_blk = QKV0 // QKV_W
    sm_blk = SM0 // LANES
    has_init = s0 is not None

    def fwd(i):
        return i

    def bwd(i):
        return n_t - 1 - i

    def tile_specs(pos):
        return [
            pl.BlockSpec((1, tl, QKV_W), lambda bb, i: (bb, pos(i), qkv_blk)),
            pl.BlockSpec((1, HALO, QKV_W), lambda bb, i: (bb, jnp.maximum(pos(i) * hb - 1, 0), qkv_blk)),
            pl.BlockSpec((1, HALO, QKV_W), lambda bb, i: (bb, jnp.minimum((pos(i) + 1) * hb, n_hb - 1), qkv_blk)),
            pl.BlockSpec((1, tl, LANES), lambda bb, i: (bb, pos(i), sm_blk)),
        ]

    tri, lvls = tables
    n_lvl = lvls.shape[1]
    in_specs = tile_specs(fwd) + tile_specs(bwd) + [
        pl.BlockSpec((None, DN_CONV_K, QKV_W), lambda bb, i: (layer, 0, 0)),
        pl.BlockSpec((None, 1, LANES), lambda bb, i: (layer, 0, 0)),
        pl.BlockSpec((None, 1, LANES), lambda bb, i: (layer, 0, 0)),
        pl.BlockSpec((2, tl, tl), lambda bb, i: (0, 0, 0)),
        pl.BlockSpec((2, n_lvl, tl, tl), lambda bb, i: (0, 0, 0, 0)),
    ]
    args = [y] * 8 + [conv_w, alog_row, dtb_row, tri, lvls]
    if has_init:
        in_specs.append(pl.BlockSpec((1, None, 2, DN_HEADS, DN_DK, DN_DV), lambda bb, i: (bb, layer, 0, 0, 0, 0)))
        args.append(s0)
    return pl.pallas_call(
        functools.partial(_gdn_kernel, tl=tl, n_t=n_t, has_init=has_init),
        grid=(b, n_t),
        in_specs=in_specs,
        out_specs=[
            pl.BlockSpec((1, tl, DN_VW), lambda bb, i: (bb, i, 0)),
            pl.BlockSpec((1, tl, DN_VW), lambda bb, i: (bb, n_t - 1 - i, 0)),
            pl.BlockSpec((1, 2, DN_HEADS, DN_DK, DN_DV), lambda bb, i: (bb, 0, 0, 0, 0)),
        ],
        out_shape=[
            jax.ShapeDtypeStruct((b, l, DN_VW), BF16),
            jax.ShapeDtypeStruct((b, l, DN_VW), BF16),
            jax.ShapeDtypeStruct((b, 2, DN_HEADS, DN_DK, DN_DV), F32),
        ],
        scratch_shapes=[
            pltpu.VMEM((tl + 2 * HALO, QKV_W), F32),
            pltpu.VMEM((tl + 2 * HALO, QKV_W), F32),
            pltpu.VMEM((tl, QKV_W), F32),
            pltpu.VMEM((tl, QKV_W), F32),
            pltpu.VMEM((2, DN_HEADS, DN_DK, DN_DV), F32),
        ],
        compiler_params=_cparams(("parallel", "arbitrary")),
        name="gdn",
    )(*args)


def _rms(x, g):
    return x * lax.rsqrt(jnp.mean(x * x, -1, keepdims=True) + EPS) * g


def _rope128(x, cos, sin):
    lane = lax.broadcasted_iota(jnp.int32, x.shape, 1)
    first = (lane % (2 * AXIS_FREQS)) < AXIS_FREQS
    sw = jnp.where(first, pltpu.roll(x, LANES - AXIS_FREQS, 1), pltpu.roll(x, AXIS_FREQS, 1))
    return x * cos + sw * sin


def _mla_prep_kernel(*refs, rope, emit_cache):
    cq_ref, ckv_ref, kr_ref, qg_ref, kvg_ref, wqn_ref, wqr_ref, wkn_ref, wv_ref = refs[:9]
    k = 9
    if rope:
        cos_ref, sin_ref = refs[k:k + 2]
        k += 2
    q_out, k_out, v_out = refs[k:k + 3]
    k += 3
    scale = (MLA_NOPE + MLA_ROPE) ** -0.5
    cqn = _rms(cq_ref[0].astype(F32), qg_ref[...]).astype(BF16)
    qn = _dot(cqn, wqn_ref[...]) * scale
    qr = _dot(cqn, wqr_ref[...]) * scale
    ckvn = _rms(ckv_ref[0].astype(F32), kvg_ref[...])
    ckvn16 = ckvn.astype(BF16)
    kn = _dot(ckvn16, wkn_ref[...])
    v_out[0] = _dot(ckvn16, wv_ref[...]).T.astype(v_out.dtype)
    kr = kr_ref[0].astype(F32)
    if emit_cache:
        ckvn_out, kr_out = refs[k:k + 2]
        ckvn_out[0] = ckvn
        kr_out[0] = kr[:, :MLA_ROPE]
    if rope:
        cos = cos_ref[...]
        sin = sin_ref[...]
        kr = _rope128(kr, cos, sin)
    kr16 = kr.astype(k_out.dtype)
    for h in range(MLA_HEADS):
        qrh = qr[:, h * LANES:(h + 1) * LANES]
        if rope:
            qrh = _rope128(qrh, cos, sin)
        q_out[0, :, h * 256:h * 256 + 128] = qn[:, h * 128:(h + 1) * 128].astype(q_out.dtype)
        q_out[0, :, h * 256 + 128:(h + 1) * 256] = qrh.astype(q_out.dtype)
        k_out[0, :, h * 256:h * 256 + 128] = kn[:, h * 128:(h + 1) * 128].astype(k_out.dtype)
        k_out[0, :, h * 256 + 128:(h + 1) * 256] = kr16


def _mla_prep(y, q_norm, kv_norm, wqn, wqr, wkn, wv, layer, rope_tabs, emit_cache):
    b, l, _ = y.shape
    tm = 256
    rope = rope_tabs is not None
    w_spec = lambda shp: pl.BlockSpec((None,) + shp, lambda bb, i: (layer, 0, 0))
    in_specs = [
        pl.BlockSpec((1, tm, Q_LORA), lambda bb, i: (bb, i, CQ0 // Q_LORA)),
        pl.BlockSpec((1, tm, KV_LORA), lambda bb, i: (bb, i, CKV0 // KV_LORA)),
        pl.BlockSpec((1, tm, LANES), lambda bb, i: (bb, i, KR0 // LANES)),
        w_spec((1, Q_LORA)), w_spec((1, KV_LORA)),
        w_spec((Q_LORA, 512)), w_spec((Q_LORA, 512)), w_spec((KV_LORA, 512)), w_spec((KV_LORA, 512)),
    ]
    args = [y, y, y, q_norm, kv_norm, wqn, wqr, wkn, wv]
    if rope:
        in_specs += [pl.BlockSpec((tm, LANES), lambda bb, i: (i, 0))] * 2
        args += list(rope_tabs)
    out_specs = [
        pl.BlockSpec((1, tm, 1024), lambda bb, i: (bb, i, 0)),
        pl.BlockSpec((1, tm, 1024), lambda bb, i: (bb, i, 0)),
        pl.BlockSpec((1, 512, tm), lambda bb, i: (bb, 0, i)),
    ]
    out_shape = [
        jax.ShapeDtypeStruct((b, l, 1024), BF16),
        jax.ShapeDtypeStruct((b, l, 1024), BF16),
        jax.ShapeDtypeStruct((b, 512, l), BF16),
    ]
    if emit_cache:
        out_specs += [pl.BlockSpec((1, tm, KV_LORA), lambda bb, i: (bb, i, 0)),
                      pl.BlockSpec((1, tm, MLA_ROPE), lambda bb, i: (bb, i, 0))]
        out_shape += [jax.ShapeDtypeStruct((b, l, KV_LORA), F32),
                      jax.ShapeDtypeStruct((b, l, MLA_ROPE), F32)]
    return pl.pallas_call(
        functools.partial(_mla_prep_kernel, rope=rope, emit_cache=emit_cache),
        grid=(b, l // tm),
        in_specs=in_specs,
        out_specs=out_specs,
        out_shape=out_shape,
        compiler_params=_cparams(("parallel", "parallel")),
        name="mla_prep",
    )(*args)


def _mla_ctx_kernel(ckv_ref, kr_ref, wkn_ref, wv_ref, k_out, v_out):
    c16 = ckv_ref[0].astype(BF16)
    kn = _dot(c16, wkn_ref[...])
    v_out[0] = _dot(c16, wv_ref[...]).T.astype(v_out.dtype)
    kr16 = kr_ref[0].astype(k_out.dtype)
    zeros = jnp.zeros((kr16.shape[0], LANES - MLA_ROPE), k_out.dtype)
    for h in range(MLA_HEADS):
        k_out[0, :, h * 256:h * 256 + 128] = kn[:, h * 128:(h + 1) * 128].astype(k_out.dtype)
        k_out[0, :, h * 256 + 128:h * 256 + 128 + MLA_ROPE] = kr16
        k_out[0, :, h * 256 + 128 + MLA_ROPE:(h + 1) * 256] = zeros


def _mla_ctx(cache_ckv, cache_krope, wkn, wv, layer):
    b, _, p, _ = cache_ckv.shape
    return pl.pallas_call(
        _mla_ctx_kernel,
        grid=(b,),
        in_specs=[
            pl.BlockSpec((1, None, p, KV_LORA), lambda bb: (bb, layer, 0, 0)),
            pl.BlockSpec((1, None, p, MLA_ROPE), lambda bb: (bb, layer, 0, 0)),
            pl.BlockSpec((None, KV_LORA, 512), lambda bb: (layer, 0, 0)),
            pl.BlockSpec((None, KV_LORA, 512), lambda bb: (layer, 0, 0)),
        ],
        out_specs=[pl.BlockSpec((1, p, 1024), lambda bb: (bb, 0, 0)),
                   pl.BlockSpec((1, 512, p), lambda bb: (bb, 0, 0))],
        out_shape=[jax.ShapeDtypeStruct((b, p, 1024), BF16),
                   jax.ShapeDtypeStruct((b, 512, p), BF16)],
        compiler_params=_cparams(("parallel",)),
        name="mla_ctx",
    )(cache_ckv, cache_krope, wkn, wv)


def _attn_kernel(*refs, has_ctx, nk):
    q_ref = refs[0]
    if has_ctx:
        kc_ref, vc_ref, k_ref, v_ref, o_ref, m_ref, l_ref, acc_ref = refs[1:]
    else:
        k_ref, v_ref, o_ref, m_ref, l_ref, acc_ref = refs[1:]
    kk = pl.program_id(2)

    @pl.when(kk == 0)
    def _():
        m_ref[...] = jnp.full(m_ref.shape, -jnp.inf, F32)
        l_ref[...] = jnp.zeros(l_ref.shape, F32)
        acc_ref[...] = jnp.zeros(acc_ref.shape, F32)

    def body(kr, vtr):
        heads = range(MLA_HEADS)
        st = [_dot_nt(kr[0, :, h * 256:(h + 1) * 256], q_ref[0, :, h * 256:(h + 1) * 256]) for h in heads]
        m_prev = [m_ref[h] for h in heads]
        m_new = [jnp.maximum(m_prev[h], jnp.max(st[h], 0, keepdims=True)) for h in heads]
        alpha = [jnp.exp(m_prev[h] - m_new[h]) for h in heads]
        p = [jnp.exp(st[h] - m_new[h]) for h in heads]
        for h in heads:
            l_ref[h] = alpha[h] * l_ref[h] + jnp.sum(p[h], 0, keepdims=True)
            m_ref[h] = m_new[h]
        for h in heads:
            rows = slice(h * MLA_V, (h + 1) * MLA_V)
            acc_ref[rows, :] = alpha[h] * acc_ref[rows, :] + _dot(vtr[0, rows, :], p[h].astype(BF16))

    if has_ctx:
        @pl.when(kk == 0)
        def _():
            body(kc_ref, vc_ref)

        @pl.when(kk > 0)
        def _():
            body(k_ref, v_ref)
    else:
        body(k_ref, v_ref)

    @pl.when(kk == nk - 1)
    def _():
        for h in range(MLA_HEADS):
            hc = slice(h * MLA_V, (h + 1) * MLA_V)
            o_ref[0, :, hc] = (acc_ref[hc, :] / l_ref[h]).T.astype(o_ref.dtype)


def _attention(q, k, vt, ctx_kv, tq, tk):
    b, l, _ = q.shape
    has_ctx = ctx_kv is not None
    n_lat = l // tk
    nk = n_lat + (1 if has_ctx else 0)
    in_specs = [pl.BlockSpec((1, tq, 1024), lambda bb, i, j: (bb, i, 0))]
    args = [q]
    if has_ctx:
        assert ctx_kv[0].shape[1] == tk
        in_specs += [pl.BlockSpec((1, tk, 1024), lambda bb, i, j: (bb, 0, 0)),
                     pl.BlockSpec((1, 512, tk), lambda bb, i, j: (bb, 0, 0))]
        args += list(ctx_kv)
        kv_blk = lambda j: jnp.maximum(j - 1, 0)
    else:
        kv_blk = lambda j: j
    in_specs += [pl.BlockSpec((1, tk, 1024), lambda bb, i, j: (bb, kv_blk(j), 0)),
                 pl.BlockSpec((1, 512, tk), lambda bb, i, j: (bb, 0, kv_blk(j)))]
    args += [k, vt]
    return pl.pallas_call(
        functools.partial(_attn_kernel, has_ctx=has_ctx, nk=nk),
        grid=(b, l // tq, nk),
        in_specs=in_specs,
        out_specs=pl.BlockSpec((1, tq, 512), lambda bb, i, j: (bb, i, 0)),
        out_shape=jax.ShapeDtypeStruct((b, l, 512), BF16),
        scratch_shapes=[
            pltpu.VMEM((MLA_HEADS, 1, tq), F32),
            pltpu.VMEM((MLA_HEADS, 1, tq), F32),
            pltpu.VMEM((MLA_HEADS * MLA_V, tq), F32),
        ],
        compiler_params=_cparams(("parallel", "parallel", "arbitrary")),
        name="attn",
    )(*args)


def _merge_kernel(oa_ref, of_ref, ob_ref, dz_ref, oc_ref, gate_ref, x_ref, mod_ref, dng_ref,
                  wpa_ref, wpb_ref, wpc_ref, wout_ref, lng_ref, lnb_ref, o_ref):
    dng = dng_ref[...]
    s = of_ref[0].astype(F32) + ob_ref[0].astype(F32)
    dz = dz_ref[0].astype(F32)
    parts = []
    for h in range(DN_HEADS):
        hc = slice(h * DN_DV, (h + 1) * DN_DV)
        parts.append((_rms(s[:, hc], dng) * jax.nn.silu(dz[:, hc])).astype(BF16))
    o_dn = jnp.concatenate(parts, axis=1)
    gates = gate_ref[0]
    ga = jax.nn.sigmoid(gates[:, 0:D_MODEL].astype(F32))
    merged = ga * _dot(oa_ref[0], wpa_ref[...])
    gb = jax.nn.sigmoid(gates[:, D_MODEL:2 * D_MODEL].astype(F32))
    merged = merged + gb * _dot(o_dn, wpb_ref[...])
    gc = jax.nn.sigmoid(gates[:, 2 * D_MODEL:3 * D_MODEL].astype(F32))
    merged = merged + gc * _dot(oc_ref[0], wpc_ref[...])
    mix = _dot(merged.astype(BF16), wout_ref[...])
    g1 = mod_ref[0, 2:3, :]
    r = DN_ALPHA * x_ref[0] + g1 * mix
    o_ref[0] = _layer_norm(r, lng_ref[...], lnb_ref[...])


def _merge(o_a, o_f, o_b, y, o_c, x, mod, dn_norm, w_pa, w_pb, w_pc, w_out, ln_g, ln_b, layer):
    bm, lm, _ = x.shape
    tm = 512
    tok = lambda w, blk: pl.BlockSpec((1, tm, w), lambda b, i: (b, i, blk))
    wsp = lambda shp: pl.BlockSpec((None,) + shp, lambda b, i: (layer, 0, 0))
    return pl.pallas_call(
        _merge_kernel,
        grid=(bm, lm // tm),
        in_specs=[
            tok(512, 0), tok(512, 0), tok(512, 0), tok(512, DZ0 // 512), tok(512, 0),
            tok(3 * D_MODEL, 0), tok(D_MODEL, 0),
            pl.BlockSpec((1, 6, D_MODEL), lambda b, i: (b, 0, 0)),
            wsp((1, DN_DV)),
            wsp((A_WIDTH, D_MODEL)), wsp((DN_VW, D_MODEL)), wsp((512, D_MODEL)), wsp((D_MODEL, D_MODEL)),
            pl.BlockSpec((None, None, 1, D_MODEL), lambda b, i: (layer, 0, 0, 0)),
            pl.BlockSpec((None, None, 1, D_MODEL), lambda b, i: (layer, 0, 0, 0)),
        ],
        out_specs=tok(D_MODEL, 0),
        out_shape=jax.ShapeDtypeStruct((bm, lm, D_MODEL), F32),
        compiler_params=_cparams(("parallel", "parallel")),
        name="merge",
    )(o_a, o_f, o_b, y, o_c, y, x, mod, dn_norm, w_pa, w_pb, w_pc, w_out, ln_g, ln_b)


def _ffn_kernel(x_ref, mod_ref, wg_ref, wu_ref, wd_ref, lng_ref, lnb_ref, o_ref, h_ref, acc_ref, *, nf):
    f = pl.program_id(2)

    @pl.when(f == 0)
    def _():
        h_ref[...] = (x_ref[0] * (1.0 + mod_ref[0, 4:5, :]) + mod_ref[0, 3:4, :]).astype(BF16)
        acc_ref[...] = jnp.zeros(acc_ref.shape, F32)

    h = h_ref[...]
    act = (jax.nn.silu(_dot(h, wg_ref[...])) * _dot(h, wu_ref[...])).astype(BF16)
    acc_ref[...] += _dot(act, wd_ref[...])

    @pl.when(f == nf - 1)
    def _():
        r = DN_ALPHA * x_ref[0] + mod_ref[0, 5:6, :] * acc_ref[...]
        o_ref[0] = _layer_norm(r, lng_ref[...], lnb_ref[...])


def _ffn(x, mod, w_gu, w_down, ln_g, ln_b, layer, idx):
    bm, lm, _ = x.shape
    tm, tf = 1024, 256
    nf = D_FF // tf
    return pl.pallas_call(
        functools.partial(_ffn_kernel, nf=nf),
        grid=(bm, lm // tm, nf),
        in_specs=[
            pl.BlockSpec((1, tm, D_MODEL), lambda b, i, f: (b, i, 0)),
            pl.BlockSpec((1, 6, D_MODEL), lambda b, i, f: (b, 0, 0)),
            pl.BlockSpec((None, D_MODEL, tf), lambda b, i, f: (idx, 0, f)),
            pl.BlockSpec((None, D_MODEL, tf), lambda b, i, f: (idx, 0, nf + f)),
            pl.BlockSpec((None, tf, D_MODEL), lambda b, i, f: (idx, f, 0)),
            pl.BlockSpec((None, None, 1, D_MODEL), lambda b, i, f: (layer, 1, 0, 0)),
            pl.BlockSpec((None, None, 1, D_MODEL), lambda b, i, f: (layer, 1, 0, 0)),
        ],
        out_specs=pl.BlockSpec((1, tm, D_MODEL), lambda b, i, f: (b, i, 0)),
        out_shape=jax.ShapeDtypeStruct((bm, lm, D_MODEL), F32),
        scratch_shapes=[pltpu.VMEM((tm, D_MODEL), BF16), pltpu.VMEM((tm, D_MODEL), F32)],
        compiler_params=_cparams(("parallel", "parallel", "arbitrary")),
        name="ffn",
    )(x, mod, w_gu, w_gu, w_down, ln_g, ln_b)


def _route(h, wr, br):
    logits = _dot_f32(h, wr)
    lane = lax.broadcasted_iota(jnp.int32, logits.shape, 1)
    valid = lane < N_EXPERTS
    neg = -jnp.inf
    sel = jnp.where(valid, logits + br, neg)
    m1 = jnp.max(sel, -1, keepdims=True)
    i1 = jnp.min(jnp.where(sel == m1, lane, LANES), -1, keepdims=True)
    sel2 = jnp.where(lane == i1, neg, sel)
    m2 = jnp.max(sel2, -1, keepdims=True)
    i2 = jnp.min(jnp.where(sel2 == m2, lane, LANES), -1, keepdims=True)
    l1 = jnp.sum(jnp.where(lane == i1, logits, 0.0), -1, keepdims=True)
    l2 = jnp.sum(jnp.where(lane == i2, logits, 0.0), -1, keepdims=True)
    mx = jnp.maximum(l1, l2)
    e1 = jnp.exp(l1 - mx)
    e2 = jnp.exp(l2 - mx)
    den = e1 + e2
    return jnp.where(lane == i1, e1 / den, 0.0) + jnp.where(lane == i2, e2 / den, 0.0)


def _moe_kernel(x_ref, mod_ref, wr_ref, br_ref, wg_ref, wu_ref, wd_ref, lng_ref, lnb_ref, o_ref,
                h_ref, comb_ref, acc_ref):
    e = pl.program_id(2)

    @pl.when(e == 0)
    def _():
        hf = x_ref[0] * (1.0 + mod_ref[0, 4:5, :]) + mod_ref[0, 3:4, :]
        h_ref[...] = hf.astype(BF16)
        comb_ref[...] = _route(hf, wr_ref[...], br_ref[...])
        acc_ref[...] = jnp.zeros(acc_ref.shape, F32)

    h = h_ref[...]
    act = (jax.nn.silu(_dot(h, wg_ref[...])) * _dot(h, wu_ref[...])).astype(BF16)
    lane = lax.broadcasted_iota(jnp.int32, comb_ref.shape, 1)
    ce = jnp.sum(jnp.where(lane == e, comb_ref[...], 0.0), -1, keepdims=True)
    acc_ref[...] += ce * _dot(act, wd_ref[...])

    @pl.when(e == N_EXPERTS - 1)
    def _():
        r = DN_ALPHA * x_ref[0] + mod_ref[0, 5:6, :] * acc_ref[...]
        o_ref[0] = _layer_norm(r, lng_ref[...], lnb_ref[...])


def _moe(x, mod, w_router_p, b_router_p, w_gu, w_down, ln_g, ln_b, layer, idx):
    bm, lm, _ = x.shape
    tm = 512
    return pl.pallas_call(
        _moe_kernel,
        grid=(bm, lm // tm, N_EXPERTS),
        in_specs=[
            pl.BlockSpec((1, tm, D_MODEL), lambda b, i, e: (b, i, 0)),
            pl.BlockSpec((1, 6, D_MODEL), lambda b, i, e: (b, 0, 0)),
            pl.BlockSpec((None, D_MODEL, LANES), lambda b, i, e: (idx, 0, 0)),
            pl.BlockSpec((None, 1, LANES), lambda b, i, e: (idx, 0, 0)),
            pl.BlockSpec((None, None, D_MODEL, D_FF_EXPERT), lambda b, i, e: (idx, e, 0, 0)),
            pl.BlockSpec((None, None, D_MODEL, D_FF_EXPERT), lambda b, i, e: (idx, e, 0, 1)),
            pl.BlockSpec((None, None, D_FF_EXPERT, D_MODEL), lambda b, i, e: (idx, e, 0, 0)),
            pl.BlockSpec((None, None, 1, D_MODEL), lambda b, i, e: (layer, 1, 0, 0)),
            pl.BlockSpec((None, None, 1, D_MODEL), lambda b, i, e: (layer, 1, 0, 0)),
        ],
        out_specs=pl.BlockSpec((1, tm, D_MODEL), lambda b, i, e: (b, i, 0)),
        out_shape=jax.ShapeDtypeStruct((bm, lm, D_MODEL), F32),
        scratch_shapes=[pltpu.VMEM((tm, D_MODEL), BF16), pltpu.VMEM((tm, LANES), F32),
                        pltpu.VMEM((tm, D_MODEL), F32)],
        compiler_params=_cparams(("parallel", "parallel", "arbitrary")),
        name="moe",
    )(x, mod, w_router_p, b_router_p, w_gu, w_gu, w_down, ln_g, ln_b)


def _pack_w_in(w_in):
    d = w_in.shape[0]
    o = 0
    cols = {}
    for name, width in (("a_u", A_WIDTH), ("a_v", A_WIDTH), ("dq", DN_QK), ("dk", DN_QK), ("dv", DN_VW),
                        ("dz", DN_VW), ("sm", 4 * DN_HEADS), ("cq", Q_LORA), ("ckv", KV_LORA), ("kr", MLA_ROPE),
                        ("ga", D_MODEL), ("gb", D_MODEL), ("gc", D_MODEL)):
        cols[name] = w_in[:, :, o:o + width]
        o += width
    zpad = lambda n: jnp.zeros((d, D_MODEL, n), w_in.dtype)
    packed = jnp.concatenate(
        [cols["ga"], cols["gb"], cols["gc"], cols["dq"], cols["dk"], cols["dv"], cols["dz"],
         cols["a_u"], cols["a_v"], cols["cq"], cols["ckv"],
         cols["kr"], zpad(LANES - MLA_ROPE), cols["sm"], zpad(LANES - 4 * DN_HEADS)], axis=-1)
    return packed.astype(BF16)


def _rope_tables(n_tok):
    row = (jnp.arange(n_tok) // GRID_W).astype(F32)
    col = (jnp.arange(n_tok) % GRID_W).astype(F32)
    inv_freq = ROPE_BASE ** (-jnp.arange(AXIS_FREQS, dtype=F32) / AXIS_FREQS)
    ang_r = row[:, None] * inv_freq
    ang_c = col[:, None] * inv_freq
    ones = jnp.ones((n_tok, LANES - MLA_ROPE), F32)
    cos = jnp.concatenate([jnp.cos(ang_r), jnp.cos(ang_r), jnp.cos(ang_c), jnp.cos(ang_c), ones], axis=1)
    sin = jnp.concatenate([-jnp.sin(ang_r), jnp.sin(ang_r), -jnp.sin(ang_c), jnp.sin(ang_c), 0.0 * ones], axis=1)
    return cos, sin


def kernel(x_prompt, x_sample, state_dn, cache_ckv, cache_krope, c, c_ctx, w_mod, b_mod, w_in, a_ln_g, a_ln_b, a_ws, a_bs, dn_conv, dn_a_log, dn_dt_bias, dn_norm, q_norm, w_qb, kv_norm, w_kvb, w_pa, w_pb, w_pc, w_out, ln_g, ln_b, ffn_gu, ffn_down, moe_router, moe_bias, moe_gu, moe_down):
    batch, seq, _ = x_prompt.shape
    dec_batch, dec_seq, _ = x_sample.shape
    depth = w_in.shape[0]
    assert depth == DEPTH

    w_in_p = _pack_w_in(w_in)
    a_bs_t = jnp.swapaxes(a_bs, 1, 2)
    ln_a_g = a_ln_g.reshape(depth, 1, A_WIDTH)
    ln_a_b = a_ln_b.reshape(depth, 1, A_WIDTH)
    lane_pad = lambda v: jnp.pad(v.reshape(depth, 1, -1), ((0, 0), (0, 0), (0, LANES - v[0].size)))
    alog_row = lane_pad(dn_a_log)
    dtb_row = lane_pad(dn_dt_bias)
    dn_g = dn_norm.reshape(depth, 1, DN_DV)
    qg = q_norm.reshape(depth, 1, Q_LORA)
    kvg = kv_norm.reshape(depth, 1, KV_LORA)
    wq = w_qb.reshape(depth, Q_LORA, MLA_HEADS, MLA_NOPE + MLA_ROPE)
    wqn = wq[..., :MLA_NOPE].reshape(depth, Q_LORA, MLA_HEADS * MLA_NOPE).astype(BF16)
    wqr = jnp.pad(wq[..., MLA_NOPE:], ((0, 0), (0, 0), (0, 0), (0, LANES - MLA_ROPE)))
    wqr = wqr.reshape(depth, Q_LORA, MLA_HEADS * LANES).astype(BF16)
    wkv = w_kvb.reshape(depth, KV_LORA, MLA_HEADS, MLA_NOPE + MLA_V)
    wkn = wkv[..., :MLA_NOPE].reshape(depth, KV_LORA, MLA_HEADS * MLA_NOPE).astype(BF16)
    wv = wkv[..., MLA_NOPE:].reshape(depth, KV_LORA, MLA_HEADS * MLA_V).astype(BF16)
    w_pa16, w_pb16, w_pc16, w_out16 = (w.astype(BF16) for w in (w_pa, w_pb, w_pc, w_out))
    ffn_gu16, ffn_down16 = ffn_gu.astype(BF16), ffn_down.astype(BF16)
    moe_gu16, moe_down16 = moe_gu.astype(BF16), moe_down.astype(BF16)
    n_moe = moe_router.shape[0]
    wr_p = jnp.pad(moe_router, ((0, 0), (0, 0), (0, LANES - N_EXPERTS)))
    br_p = jnp.pad(moe_bias.reshape(n_moe, 1, N_EXPERTS), ((0, 0), (0, 0), (0, LANES - N_EXPERTS)))
    ln_g4 = ln_g.reshape(depth, 2, 1, D_MODEL)
    ln_b4 = ln_b.reshape(depth, 2, 1, D_MODEL)
    rope_tabs = _rope_tables(dec_seq)
    gdn_tabs = _gdn_tables(GDN_TILE)

    n_cond = 16
    cond = jnp.concatenate([c, c_ctx[None, :], jnp.zeros((n_cond - dec_batch - 1, D_MODEL), F32)], axis=0)
    mods = _adaln(cond, w_mod, b_mod).reshape(depth, n_cond, 6, D_MODEL)

    xp = x_prompt.reshape(1, batch * seq, D_MODEL)
    xs = x_sample
    st_dn, st_ckv, st_kr = [], [], []

    def channel_mixer(x, mod, l):
        if l % 2 == 0:
            return _ffn(x, mod, ffn_gu16, ffn_down16, ln_g4, ln_b4, l, l // 2)
        return _moe(x, mod, wr_p, br_p, moe_gu16, moe_down16, ln_g4, ln_b4, l, l // 2)

    def token_mixer(x, mod, l, bsz, n_tok, latent):
        y = _inproj(x, mod, w_in_p, l)
        o_a = _gmlp(y.reshape(-1, NP), a_ws, a_bs_t, ln_a_g, ln_a_b, l).reshape(x.shape[0], x.shape[1], A_WIDTH)
        ys = y.reshape(bsz, n_tok, NP)
        o_f, o_b, s_fin = _gdn(ys, dn_conv, alog_row, dtb_row, gdn_tabs, state_dn if latent else None, l)
        if latent:
            q, k, v = _mla_prep(ys, qg, kvg, wqn, wqr, wkn, wv, l, rope_tabs, False)
            ctx_kv = _mla_ctx(cache_ckv, cache_krope, wkn, wv, l)
            o_c = _attention(q, k, v, ctx_kv, 512, 512)
            extra = None
        else:
            q, k, v, ckv_n, kr = _mla_prep(ys, qg, kvg, wqn, wqr, wkn, wv, l, None, True)
            o_c = _attention(q, k, v, None, n_tok, n_tok)
            extra = (s_fin, ckv_n, kr)
        shp = (x.shape[0], x.shape[1], 512)
        x = _merge(o_a, o_f.reshape(shp), o_b.reshape(shp), y, o_c.reshape(shp), x, mod, dn_g,
                   w_pa16, w_pb16, w_pc16, w_out16, ln_g4, ln_b4, l)
        return x, extra

    for l in range(depth):
        mod_p = mods[l, dec_batch:dec_batch + 1]
        mod_s = mods[l, :dec_batch]
        xp, (s_fin, ckv_n, kr) = token_mixer(xp, mod_p, l, batch, seq, False)
        xp = channel_mixer(xp, mod_p, l)
        st_dn.append(s_fin)
        st_ckv.append(ckv_n)
        st_kr.append(kr)
        xs, _ = token_mixer(xs, mod_s, l, dec_batch, dec_seq, True)
        xs = channel_mixer(xs, mod_s, l)

    return (xp.reshape(batch, seq, D_MODEL), xs,
            jnp.stack(st_dn, axis=1), jnp.stack(st_ckv, axis=1), jnp.stack(st_kr, axis=1))
```

```python
import functools
import math

import jax
import jax.numpy as jnp
from jax import lax
from jax.experimental import pallas as pl
from jax.experimental.pallas import tpu as pltpu

F32 = jnp.float32
BF16 = jnp.bfloat16

D_MODEL = 1024
DEPTH = 4
GRID_W = 64
A_WIDTH = 512
A_CHUNK = 128
A_GROUPS = 4
DN_HEADS = 4
DN_DK = 128
DN_DV = 128
DN_QK = DN_HEADS * DN_DK
DN_VW = DN_HEADS * DN_DV
DN_CONV_K = 5
GDN_TILE = 256
MLA_HEADS = 4
MLA_NOPE = 128
MLA_ROPE = 64
MLA_V = 128
Q_LORA = 256
KV_LORA = 256
AXIS_FREQS = MLA_ROPE // 4
ROPE_BASE = 10000.0
D_FF = 2816
N_EXPERTS = 8
D_FF_EXPERT = D_FF // 2
DN_ALPHA = (2 * DEPTH) ** 0.25
EPS = 1e-6
LN_EPS = 1e-5

LANES = 128
HALO = 16
VMEM_LIMIT = 56 * 1024 * 1024

GATE0 = 0
QKV0 = 3 * D_MODEL
DZ0 = QKV0 + 3 * DN_QK
AU0 = DZ0 + DN_VW
AV0 = AU0 + A_WIDTH
CQ0 = AV0 + A_WIDTH
CKV0 = CQ0 + Q_LORA
KR0 = CKV0 + KV_LORA
SM0 = KR0 + LANES
NP = SM0 + LANES
QKV_W = 3 * DN_QK


def _cparams(sem):
    return pltpu.CompilerParams(dimension_semantics=sem, vmem_limit_bytes=VMEM_LIMIT)


def _dot(a, b):
    return jnp.dot(a, b, preferred_element_type=F32)


def _dot_nt(a, b):
    return lax.dot_general(a, b, (((1,), (1,)), ((), ())), preferred_element_type=F32)


def _dot_tn(a, b):
    return lax.dot_general(a, b, (((0,), (0,)), ((), ())), preferred_element_type=F32)


def _split3(x):
    hi = x.astype(BF16)
    r = x - hi.astype(F32)
    mid = r.astype(BF16)
    lo = (r - mid.astype(F32)).astype(BF16)
    return hi, mid, lo


def _dot_f32(a, b):
    ah, am, al = _split3(a)
    bh, bm, bl = _split3(b)
    return (_dot(ah, bh) + (_dot(ah, bm) + _dot(am, bh))
            + (_dot(am, bm) + _dot(ah, bl) + _dot(al, bh)))


def _layer_norm(r, g, b):
    mu = jnp.mean(r, -1, keepdims=True)
    rc = r - mu
    var = jnp.mean(rc * rc, -1, keepdims=True)
    return rc * lax.rsqrt(var + LN_EPS) * g + b


def _adaln_kernel(c_ref, w_ref, b_ref, o_ref):
    s = jax.nn.silu(c_ref[...])
    o_ref[...] = _dot_f32(s, w_ref[...]) + b_ref[...]


def _adaln(cond, w_mod, b_mod):
    rows = cond.shape[0]
    n_out = w_mod.shape[-1]
    tn = 1536
    return pl.pallas_call(
        _adaln_kernel,
        grid=(DEPTH, n_out // tn),
        in_specs=[
            pl.BlockSpec((rows, D_MODEL), lambda l, j: (0, 0)),
            pl.BlockSpec((None, D_MODEL, tn), lambda l, j: (l, 0, j)),
            pl.BlockSpec((None, 1, tn), lambda l, j: (l, 0, j)),
        ],
        out_specs=pl.BlockSpec((None, rows, tn), lambda l, j: (l, 0, j)),
        out_shape=jax.ShapeDtypeStruct((DEPTH, rows, n_out), F32),
        compiler_params=_cparams(("parallel", "parallel")),
        name="adaln",
    )(cond, w_mod, b_mod.reshape(DEPTH, 1, n_out))


def _inproj_kernel(x_ref, mod_ref, w_ref, y_ref, h_ref):
    @pl.when(pl.program_id(2) == 0)
    def _():
        sh = mod_ref[0, 0:1, :]
        sc = mod_ref[0, 1:2, :]
        h_ref[...] = (x_ref[0] * (1.0 + sc) + sh).astype(BF16)

    y_ref[0] = _dot(h_ref[...], w_ref[...]).astype(y_ref.dtype)


def _inproj(x, mod, w_in_p, layer):
    bm, lm, _ = x.shape
    tm, tn = 1024, 1152
    return pl.pallas_call(
        _inproj_kernel,
        grid=(bm, lm // tm, NP // tn),
        in_specs=[
            pl.BlockSpec((1, tm, D_MODEL), lambda b, i, j: (b, i, 0)),
            pl.BlockSpec((1, 6, D_MODEL), lambda b, i, j: (b, 0, 0)),
            pl.BlockSpec((None, D_MODEL, tn), lambda b, i, j: (layer, 0, j)),
        ],
        out_specs=pl.BlockSpec((1, tm, tn), lambda b, i, j: (b, i, j)),
        out_shape=jax.ShapeDtypeStruct((bm, lm, NP), BF16),
        scratch_shapes=[pltpu.VMEM((tm, D_MODEL), BF16)],
        compiler_params=_cparams(("parallel", "parallel", "arbitrary")),
        name="inproj",
    )(x, mod, w_in_p)


def _gmlp_kernel(u_ref, v_ref, ws_ref, bs_ref, g_ref, b_ref, o_ref):
    tl = u_ref.shape[0]
    gd = A_WIDTH // A_GROUPS
    v = jax.nn.gelu(v_ref[...].astype(F32))
    vn = _layer_norm(v, g_ref[...], b_ref[...]).astype(BF16)
    u = jax.nn.gelu(u_ref[...].astype(F32))
    for n in range(tl // A_CHUNK):
        rows = slice(n * A_CHUNK, (n + 1) * A_CHUNK)
        for g in range(A_GROUPS):
            cols = slice(g * gd, (g + 1) * gd)
            mixed = _dot(ws_ref[g].astype(BF16), vn[rows, cols]) + bs_ref[:, g:g + 1]
            o_ref[rows, cols] = (u[rows, cols] * mixed).astype(o_ref.dtype)


def _gmlp(y2d, a_ws, a_bs_t, ln_g, ln_b, layer):
    t = y2d.shape[0]
    tl = 512
    return pl.pallas_call(
        _gmlp_kernel,
        grid=(t // tl,),
        in_specs=[
            pl.BlockSpec((tl, A_WIDTH), lambda i: (i, AU0 // A_WIDTH)),
            pl.BlockSpec((tl, A_WIDTH), lambda i: (i, AV0 // A_WIDTH)),
            pl.BlockSpec((None, A_GROUPS, A_CHUNK, A_CHUNK), lambda i: (layer, 0, 0, 0)),
            pl.BlockSpec((None, A_CHUNK, A_GROUPS), lambda i: (layer, 0, 0)),
            pl.BlockSpec((None, 1, A_WIDTH), lambda i: (layer, 0, 0)),
            pl.BlockSpec((None, 1, A_WIDTH), lambda i: (layer, 0, 0)),
        ],
        out_specs=pl.BlockSpec((tl, A_WIDTH), lambda i: (i, 0)),
        out_shape=jax.ShapeDtypeStruct((t, A_WIDTH), BF16),
        compiler_params=_cparams(("parallel",)),
        name="gmlp",
    )(y2d, y2d, a_ws, a_bs_t, ln_g, ln_b)


def _gdn_stage1(d, reverse, is_first, is_last, qkv_ref, prev_ref, next_ref, sm_ref, conv_ref, alog_ref,
                dtb_ref, tri_ref, lvl_ref, ext_ref, act_ref, o_ref, tl, probs):
    c = tl
    off = HALO - DN_CONV_K // 2
    st = {}

    def fill():
        ext_ref[0:HALO, :] = jnp.where(is_first, 0.0, prev_ref[0].astype(F32))
        ext_ref[HALO:HALO + tl, :] = qkv_ref[0].astype(F32)
        ext_ref[HALO + tl:2 * HALO + tl, :] = jnp.where(is_last, 0.0, next_ref[0].astype(F32))

    def conv(part):
        cols = slice(part * DN_QK, (part + 1) * DN_QK)
        acc = conv_ref[0:1, cols] * ext_ref[off:off + tl, cols]
        for j in range(1, DN_CONV_K):
            acc = acc + conv_ref[j:j + 1, cols] * ext_ref[off + j:off + j + tl, cols]
        y = jax.nn.silu(acc)
        if part < 2:
            for h in range(DN_HEADS):
                yh = y[:, h * DN_DK:(h + 1) * DN_DK]
                yh = yh * lax.rsqrt(jnp.sum(yh * yh, -1, keepdims=True) + EPS)
                act_ref[:, part * DN_QK + h * DN_DK:part * DN_QK + (h + 1) * DN_DK] = yh
        else:
            act_ref[:, cols] = y

    def gates():
        sm = sm_ref[0].astype(F32)
        g_all = -jnp.exp(alog_ref[...]) * jax.nn.softplus(sm + dtb_ref[...])
        st["beta"] = jax.nn.sigmoid(sm)
        tri = tri_ref[d]
        gh, gm, gl = _split3(g_all)
        st["gam"] = _dot(tri, gh) + _dot(tri, gm) + _dot(tri, gl)
        st["gam_t"] = st["gam"].T
        ii = lax.broadcasted_iota(jnp.int32, (c, c), 0)
        jj = lax.broadcasted_iota(jnp.int32, (c, c), 1)
        st["incl"] = (jj >= ii) if reverse else (jj <= ii)
        st["strict"] = (jj > ii) if reverse else (jj < ii)
        st["eye"] = jnp.where(ii == jj, 1.0, 0.0).astype(F32)

    def head(h):
        scale = DN_DK ** -0.5
        incl, strict = st["incl"], st["strict"]
        lane = d * DN_HEADS + h
        gc = st["gam"][:, lane:lane + 1]
        gr = st["gam_t"][lane:lane + 1, :]
        bcol = st["beta"][:, 2 * DN_HEADS + lane:2 * DN_HEADS + lane + 1]
        decay = jnp.where(incl, jnp.exp(jnp.where(incl, gc - gr, 0.0)), 0.0)
        qh = act_ref[:, h * DN_DK:(h + 1) * DN_DK] * scale
        kh = act_ref[:, DN_QK + h * DN_DK:DN_QK + (h + 1) * DN_DK]
        vh = act_ref[:, 2 * DN_QK + h * DN_DV:2 * DN_QK + (h + 1) * DN_DV]
        kb = kh * bcol
        kh16 = kh.astype(BF16)
        low16 = jnp.where(strict, _dot_nt(kb.astype(BF16), kh16) * decay, 0.0).astype(BF16)
        qk16 = jnp.where(incl, _dot_nt(qh.astype(BF16), kh16) * decay, 0.0).astype(BF16)
        eg = jnp.exp(gc)
        g_last = gc[0:1, :] if reverse else gc[c - 1:c, :]
        probs.append(dict(
            d=d, h=h, o_ref=o_ref, low16=low16, qk16=qk16,
            rhs16=jnp.concatenate([vh * bcol, kb * eg], axis=1).astype(BF16),
            qe16=(qh * eg).astype(BF16),
            kdec16=(kh * jnp.exp(g_last - gc)).astype(BF16),
            e_last=jnp.exp(g_last),
            inv=st["eye"] - (low16 * lvl_ref[d, 0]).astype(F32)))

    return ([lambda: (fill(), conv(0)), lambda: conv(1), lambda: (conv(2), gates())]
            + [functools.partial(head, h) for h in range(DN_HEADS)])


def _gdn_level(probs, lvl_ref, lv):
    for p in probs:
        p["inv16"] = p["inv"].astype(BF16)
        p["t16"] = _dot(p["low16"] * lvl_ref[p["d"], lv], p["inv16"]).astype(BF16)
    for p in probs:
        p["inv"] = p["inv"] - _dot(p["inv16"], p["t16"])


def _gdn_solve(p):
    sol = _dot(p["inv"].astype(BF16), p["rhs16"])
    p["u"] = sol[:, :DN_DV]
    p["w16"] = sol[:, DN_DV:].astype(BF16)


def _gdn_scan(p, s_ref):
    st = s_ref[p["d"], p["h"]]
    st16 = st.astype(BF16)
    v16 = (p["u"] - _dot(p["w16"], st16)).astype(BF16)
    o = _dot(p["qe16"], st16) + _dot(p["qk16"], v16)
    s_ref[p["d"], p["h"]] = st * p["e_last"] + _dot_tn(p["kdec16"], v16)
    p["o_ref"][0, :, p["h"] * DN_DV:(p["h"] + 1) * DN_DV] = o.astype(p["o_ref"].dtype)


def _gdn_kernel(*refs, tl, n_t, has_init):
    (qkv_f, prev_f, next_f, sm_f, qkv_b, prev_b, next_b, sm_b,
     conv_ref, alog_ref, dtb_ref, tri_ref, lvl_ref) = refs[:13]
    k = 13
    if has_init:
        s0_ref = refs[k]
        k += 1
    o_f, o_b, sfin_ref, ext_f, ext_b, act_f, act_b, s_ref = refs[k:]
    i = pl.program_id(1)
    n_lvl = lvl_ref.shape[1]

    @pl.when(i == 0)
    def _():
        if has_init:
            s_ref[...] = s0_ref[0]
        else:
            s_ref[...] = jnp.zeros(s_ref.shape, F32)

    probs_f, probs_b = [], []
    items_f = _gdn_stage1(0, False, i == 0, i == n_t - 1, qkv_f, prev_f, next_f, sm_f, conv_ref, alog_ref,
                          dtb_ref, tri_ref, lvl_ref, ext_f, act_f, o_f, tl, probs_f)
    items_b = _gdn_stage1(1, True, i == n_t - 1, i == 0, qkv_b, prev_b, next_b, sm_b, conv_ref, alog_ref,
                          dtb_ref, tri_ref, lvl_ref, ext_b, act_b, o_b, tl, probs_b)
    for item in items_f + items_b:
        item()
    probs = probs_f + probs_b
    for lv in range(1, n_lvl):
        _gdn_level(probs, lvl_ref, lv)
    for p in probs:
        _gdn_solve(p)
    for p in probs:
        _gdn_scan(p, s_ref)

    @pl.when(i == n_t - 1)
    def _():
        sfin_ref[0] = s_ref[...]


def _gdn_tables(c):
    ii = jnp.arange(c)[:, None]
    jj = jnp.arange(c)[None, :]
    tri = jnp.stack([jj <= ii, jj >= ii]).astype(BF16)
    x = ii ^ jj
    lvls = []
    s = 1
    while s < c:
        pair = (x >= s) & (x < 2 * s)
        lvls.append(jnp.stack([pair & (jj < ii), pair & (jj > ii)]))
        s *= 2
    return tri, jnp.stack(lvls, axis=1).astype(BF16)


def _gdn(y, conv_w, alog_row, dtb_row, tables, s0, layer):
    b, l, _ = y.shape
    tl = GDN_TILE
    n_t = l // tl
    hb = tl // HALO
    n_hb = l // HALO
    qkv_blk = QKV0 // QKV_W
    sm_blk = SM0 // LANES
    has_init = s0 is not None

    def fwd(i):
        return i

    def bwd(i):
        return n_t - 1 - i

    def tile_specs(pos):
        return [
            pl.BlockSpec((1, tl, QKV_W), lambda bb, i: (bb, pos(i), qkv_blk)),
            pl.BlockSpec((1, HALO, QKV_W), lambda bb, i: (bb, jnp.maximum(pos(i) * hb - 1, 0), qkv_blk)),
            pl.BlockSpec((1, HALO, QKV_W), lambda bb, i: (bb, jnp.minimum((pos(i) + 1) * hb, n_hb - 1), qkv_blk)),
            pl.BlockSpec((1, tl, LANES), lambda bb, i: (bb, pos(i), sm_blk)),
        ]

    tri, lvls = tables
    n_lvl = lvls.shape[1]
    in_specs = tile_specs(fwd) + tile_specs(bwd) + [
        pl.BlockSpec((None, DN_CONV_K, QKV_W), lambda bb, i: (layer, 0, 0)),
        pl.BlockSpec((None, 1, LANES), lambda bb, i: (layer, 0, 0)),
        pl.BlockSpec((None, 1, LANES), lambda bb, i: (layer, 0, 0)),
        pl.BlockSpec((2, tl, tl), lambda bb, i: (0, 0, 0)),
        pl.BlockSpec((2, n_lvl, tl, tl), lambda bb, i: (0, 0, 0, 0)),
    ]
    args = [y] * 8 + [conv_w, alog_row, dtb_row, tri, lvls]
    if has_init:
        in_specs.append(pl.BlockSpec((1, None, 2, DN_HEADS, DN_DK, DN_DV), lambda bb, i: (bb, layer, 0, 0, 0, 0)))
        args.append(s0)
    return pl.pallas_call(
        functools.partial(_gdn_kernel, tl=tl, n_t=n_t, has_init=has_init),
        grid=(b, n_t),
        in_specs=in_specs,
        out_specs=[
            pl.BlockSpec((1, tl, DN_VW), lambda bb, i: (bb, i, 0)),
            pl.BlockSpec((1, tl, DN_VW), lambda bb, i: (bb, n_t - 1 - i, 0)),
            pl.BlockSpec((1, 2, DN_HEADS, DN_DK, DN_DV), lambda bb, i: (bb, 0, 0, 0, 0)),
        ],
        out_shape=[
            jax.ShapeDtypeStruct((b, l, DN_VW), BF16),
            jax.ShapeDtypeStruct((b, l, DN_VW), BF16),
            jax.ShapeDtypeStruct((b, 2, DN_HEADS, DN_DK, DN_DV), F32),
        ],
        scratch_shapes=[
            pltpu.VMEM((tl + 2 * HALO, QKV_W), F32),
            pltpu.VMEM((tl + 2 * HALO, QKV_W), F32),
            pltpu.VMEM((tl, QKV_W), F32),
            pltpu.VMEM((tl, QKV_W), F32),
            pltpu.VMEM((2, DN_HEADS, DN_DK, DN_DV), F32),
        ],
        compiler_params=_cparams(("parallel", "arbitrary")),
        name="gdn",
    )(*args)


def _rms(x, g):
    return x * lax.rsqrt(jnp.mean(x * x, -1, keepdims=True) + EPS) * g


def _rope128(x, cos, sin):
    lane = lax.broadcasted_iota(jnp.int32, x.shape, 1)
    first = (lane % (2 * AXIS_FREQS)) < AXIS_FREQS
    sw = jnp.where(first, pltpu.roll(x, LANES - AXIS_FREQS, 1), pltpu.roll(x, AXIS_FREQS, 1))
    return x * cos + sw * sin


def _mla_prep_kernel(*refs, rope, emit_cache):
    cq_ref, ckv_ref, kr_ref, qg_ref, kvg_ref, wqn_ref, wqr_ref, wkn_ref, wv_ref = refs[:9]
    k = 9
    if rope:
        cos_ref, sin_ref = refs[k:k + 2]
        k += 2
    q_out, k_out, v_out = refs[k:k + 3]
    k += 3
    scale = (MLA_NOPE + MLA_ROPE) ** -0.5
    cqn = _rms(cq_ref[0].astype(F32), qg_ref[...]).astype(BF16)
    qn = _dot(cqn, wqn_ref[...]) * scale
    qr = _dot(cqn, wqr_ref[...]) * scale
    ckvn = _rms(ckv_ref[0].astype(F32), kvg_ref[...])
    ckvn16 = ckvn.astype(BF16)
    kn = _dot(ckvn16, wkn_ref[...])
    v_out[0] = _dot(ckvn16, wv_ref[...]).T.astype(v_out.dtype)
    kr = kr_ref[0].astype(F32)
    if emit_cache:
        ckvn_out, kr_out = refs[k:k + 2]
        ckvn_out[0] = ckvn
        kr_out[0] = kr[:, :MLA_ROPE]
    if rope:
        cos = cos_ref[...]
        sin = sin_ref[...]
        kr = _rope128(kr, cos, sin)
    kr16 = kr.astype(k_out.dtype)
    for h in range(MLA_HEADS):
        qrh = qr[:, h * LANES:(h + 1) * LANES]
        if rope:
            qrh = _rope128(qrh, cos, sin)
        q_out[0, :, h * 256:h * 256 + 128] = qn[:, h * 128:(h + 1) * 128].astype(q_out.dtype)
        q_out[0, :, h * 256 + 128:(h + 1) * 256] = qrh.astype(q_out.dtype)
        k_out[0, :, h * 256:h * 256 + 128] = kn[:, h * 128:(h + 1) * 128].astype(k_out.dtype)
        k_out[0, :, h * 256 + 128:(h + 1) * 256] = kr16


def _mla_prep(y, q_norm, kv_norm, wqn, wqr, wkn, wv, layer, rope_tabs, emit_cache):
    b, l, _ = y.shape
    tm = 256
    rope = rope_tabs is not None
    w_spec = lambda shp: pl.BlockSpec((None,) + shp, lambda bb, i: (layer, 0, 0))
    in_specs = [
        pl.BlockSpec((1, tm, Q_LORA), lambda bb, i: (bb, i, CQ0 // Q_LORA)),
        pl.BlockSpec((1, tm, KV_LORA), lambda bb, i: (bb, i, CKV0 // KV_LORA)),
        pl.BlockSpec((1, tm, LANES), lambda bb, i: (bb, i, KR0 // LANES)),
        w_spec((1, Q_LORA)), w_spec((1, KV_LORA)),
        w_spec((Q_LORA, 512)), w_spec((Q_LORA, 512)), w_spec((KV_LORA, 512)), w_spec((KV_LORA, 512)),
    ]
    args = [y, y, y, q_norm, kv_norm, wqn, wqr, wkn, wv]
    if rope:
        in_specs += [pl.BlockSpec((tm, LANES), lambda bb, i: (i, 0))] * 2
        args += list(rope_tabs)
    out_specs = [
        pl.BlockSpec((1, tm, 1024), lambda bb, i: (bb, i, 0)),
        pl.BlockSpec((1, tm, 1024), lambda bb, i: (bb, i, 0)),
        pl.BlockSpec((1, 512, tm), lambda bb, i: (bb, 0, i)),
    ]
    out_shape = [
        jax.ShapeDtypeStruct((b, l, 1024), BF16),
        jax.ShapeDtypeStruct((b, l, 1024), BF16),
        jax.ShapeDtypeStruct((b, 512, l), BF16),
    ]
    if emit_cache:
        out_specs += [pl.BlockSpec((1, tm, KV_LORA), lambda bb, i: (bb, i, 0)),
                      pl.BlockSpec((1, tm, MLA_ROPE), lambda bb, i: (bb, i, 0))]
        out_shape += [jax.ShapeDtypeStruct((b, l, KV_LORA), F32),
                      jax.ShapeDtypeStruct((b, l, MLA_ROPE), F32)]
    return pl.pallas_call(
        functools.partial(_mla_prep_kernel, rope=rope, emit_cache=emit_cache),
        grid=(b, l // tm),
        in_specs=in_specs,
        out_specs=out_specs,
        out_shape=out_shape,
        compiler_params=_cparams(("parallel", "parallel")),
        name="mla_prep",
    )(*args)


def _mla_ctx_kernel(ckv_ref, kr_ref, wkn_ref, wv_ref, k_out, v_out):
    c16 = ckv_ref[0].astype(BF16)
    kn = _dot(c16, wkn_ref[...])
    v_out[0] = _dot(c16, wv_ref[...]).T.astype(v_out.dtype)
    kr16 = kr_ref[0].astype(k_out.dtype)
    zeros = jnp.zeros((kr16.shape[0], LANES - MLA_ROPE), k_out.dtype)
    for h in range(MLA_HEADS):
        k_out[0, :, h * 256:h * 256 + 128] = kn[:, h * 128:(h + 1) * 128].astype(k_out.dtype)
        k_out[0, :, h * 256 + 128:h * 256 + 128 + MLA_ROPE] = kr16
        k_out[0, :, h * 256 + 128 + MLA_ROPE:(h + 1) * 256] = zeros


def _mla_ctx(cache_ckv, cache_krope, wkn, wv, layer):
    b, _, p, _ = cache_ckv.shape
    return pl.pallas_call(
        _mla_ctx_kernel,
        grid=(b,),
        in_specs=[
            pl.BlockSpec((1, None, p, KV_LORA), lambda bb: (bb, layer, 0, 0)),
            pl.BlockSpec((1, None, p, MLA_ROPE), lambda bb: (bb, layer, 0, 0)),
            pl.BlockSpec((None, KV_LORA, 512), lambda bb: (layer, 0, 0)),
            pl.BlockSpec((None, KV_LORA, 512), lambda bb: (layer, 0, 0)),
        ],
        out_specs=[pl.BlockSpec((1, p, 1024), lambda bb: (bb, 0, 0)),
                   pl.BlockSpec((1, 512, p), lambda bb: (bb, 0, 0))],
        out_shape=[jax.ShapeDtypeStruct((b, p, 1024), BF16),
                   jax.ShapeDtypeStruct((b, 512, p), BF16)],
        compiler_params=_cparams(("parallel",)),
        name="mla_ctx",
    )(cache_ckv, cache_krope, wkn, wv)


def _attn_kernel(*refs, has_ctx, nk):
    q_ref = refs[0]
    if has_ctx:
        kc_ref, vc_ref, k_ref, v_ref, o_ref, m_ref, l_ref, acc_ref = refs[1:]
    else:
        k_ref, v_ref, o_ref, m_ref, l_ref, acc_ref = refs[1:]
    kk = pl.program_id(2)

    @pl.when(kk == 0)
    def _():
        m_ref[...] = jnp.full(m_ref.shape, -jnp.inf, F32)
        l_ref[...] = jnp.zeros(l_ref.shape, F32)
        acc_ref[...] = jnp.zeros(acc_ref.shape, F32)

    def body(kr, vtr):
        heads = range(MLA_HEADS)
        st = [_dot_nt(kr[0, :, h * 256:(h + 1) * 256], q_ref[0, :, h * 256:(h + 1) * 256]) for h in heads]
        m_prev = [m_ref[h] for h in heads]
        m_new = [jnp.maximum(m_prev[h], jnp.max(st[h], 0, keepdims=True)) for h in heads]
        alpha = [jnp.exp(m_prev[h] - m_new[h]) for h in heads]
        p = [jnp.exp(st[h] - m_new[h]) for h in heads]
        for h in heads:
            l_ref[h] = alpha[h] * l_ref[h] + jnp.sum(p[h], 0, keepdims=True)
            m_ref[h] = m_new[h]
        for h in heads:
            rows = slice(h * MLA_V, (h + 1) * MLA_V)
            acc_ref[rows, :] = alpha[h] * acc_ref[rows, :] + _dot(vtr[0, rows, :], p[h].astype(BF16))

    if has_ctx:
        @pl.when(kk == 0)
        def _():
            body(kc_ref, vc_ref)

        @pl.when(kk > 0)
        def _():
            body(k_ref, v_ref)
    else:
        body(k_ref, v_ref)

    @pl.when(kk == nk - 1)
    def _():
        for h in range(MLA_HEADS):
            hc = slice(h * MLA_V, (h + 1) * MLA_V)
            o_ref[0, :, hc] = (acc_ref[hc, :] / l_ref[h]).T.astype(o_ref.dtype)


def _attention(q, k, vt, ctx_kv, tq, tk):
    b, l, _ = q.shape
    has_ctx = ctx_kv is not None
    n_lat = l // tk
    nk = n_lat + (1 if has_ctx else 0)
    in_specs = [pl.BlockSpec((1, tq, 1024), lambda bb, i, j: (bb, i, 0))]
    args = [q]
    if has_ctx:
        assert ctx_kv[0].shape[1] == tk
        in_specs += [pl.BlockSpec((1, tk, 1024), lambda bb, i, j: (bb, 0, 0)),
                     pl.BlockSpec((1, 512, tk), lambda bb, i, j: (bb, 0, 0))]
        args += list(ctx_kv)
        kv_blk = lambda j: jnp.maximum(j - 1, 0)
    else:
        kv_blk = lambda j: j
    in_specs += [pl.BlockSpec((1, tk, 1024), lambda bb, i, j: (bb, kv_blk(j), 0)),
                 pl.BlockSpec((1, 512, tk), lambda bb, i, j: (bb, 0, kv_blk(j)))]
    args += [k, vt]
    return pl.pallas_call(
        functools.partial(_attn_kernel, has_ctx=has_ctx, nk=nk),
        grid=(b, l // tq, nk),
        in_specs=in_specs,
        out_specs=pl.BlockSpec((1, tq, 512), lambda bb, i, j: (bb, i, 0)),
        out_shape=jax.ShapeDtypeStruct((b, l, 512), BF16),
        scratch_shapes=[
            pltpu.VMEM((MLA_HEADS, 1, tq), F32),
            pltpu.VMEM((MLA_HEADS, 1, tq), F32),
            pltpu.VMEM((MLA_HEADS * MLA_V, tq), F32),
        ],
        compiler_params=_cparams(("parallel", "parallel", "arbitrary")),
        name="attn",
    )(*args)


def _merge_kernel(oa_ref, of_ref, ob_ref, dz_ref, oc_ref, gate_ref, x_ref, mod_ref, dng_ref,
                  wpa_ref, wpb_ref, wpc_ref, wout_ref, lng_ref, lnb_ref, o_ref):
    dng = dng_ref[...]
    s = of_ref[0].astype(F32) + ob_ref[0].astype(F32)
    dz = dz_ref[0].astype(F32)
    parts = []
    for h in range(DN_HEADS):
        hc = slice(h * DN_DV, (h + 1) * DN_DV)
        parts.append((_rms(s[:, hc], dng) * jax.nn.silu(dz[:, hc])).astype(BF16))
    o_dn = jnp.concatenate(parts, axis=1)
    gates = gate_ref[0]
    ga = jax.nn.sigmoid(gates[:, 0:D_MODEL].astype(F32))
    merged = ga * _dot(oa_ref[0], wpa_ref[...])
    gb = jax.nn.sigmoid(gates[:, D_MODEL:2 * D_MODEL].astype(F32))
    merged = merged + gb * _dot(o_dn, wpb_ref[...])
    gc = jax.nn.sigmoid(gates[:, 2 * D_MODEL:3 * D_MODEL].astype(F32))
    merged = merged + gc * _dot(oc_ref[0], wpc_ref[...])
    mix = _dot(merged.astype(BF16), wout_ref[...])
    g1 = mod_ref[0, 2:3, :]
    r = DN_ALPHA * x_ref[0] + g1 * mix
    o_ref[0] = _layer_norm(r, lng_ref[...], lnb_ref[...])


def _merge(o_a, o_f, o_b, y, o_c, x, mod, dn_norm, w_pa, w_pb, w_pc, w_out, ln_g, ln_b, layer):
    bm, lm, _ = x.shape
    tm = 512
    tok = lambda w, blk: pl.BlockSpec((1, tm, w), lambda b, i: (b, i, blk))
    wsp = lambda shp: pl.BlockSpec((None,) + shp, lambda b, i: (layer, 0, 0))
    return pl.pallas_call(
        _merge_kernel,
        grid=(bm, lm // tm),
        in_specs=[
            tok(512, 0), tok(512, 0), tok(512, 0), tok(512, DZ0 // 512), tok(512, 0),
            tok(3 * D_MODEL, 0), tok(D_MODEL, 0),
            pl.BlockSpec((1, 6, D_MODEL), lambda b, i: (b, 0, 0)),
            wsp((1, DN_DV)),
            wsp((A_WIDTH, D_MODEL)), wsp((DN_VW, D_MODEL)), wsp((512, D_MODEL)), wsp((D_MODEL, D_MODEL)),
            pl.BlockSpec((None, None, 1, D_MODEL), lambda b, i: (layer, 0, 0, 0)),
            pl.BlockSpec((None, None, 1, D_MODEL), lambda b, i: (layer, 0, 0, 0)),
        ],
        out_specs=tok(D_MODEL, 0),
        out_shape=jax.ShapeDtypeStruct((bm, lm, D_MODEL), F32),
        compiler_params=_cparams(("parallel", "parallel")),
        name="merge",
    )(o_a, o_f, o_b, y, o_c, y, x, mod, dn_norm, w_pa, w_pb, w_pc, w_out, ln_g, ln_b)


def _ffn_kernel(x_ref, mod_ref, wg_ref, wu_ref, wd_ref, lng_ref, lnb_ref, o_ref, h_ref, acc_ref, *, nf):
    f = pl.program_id(2)

    @pl.when(f == 0)
    def _():
        h_ref[...] = (x_ref[0] * (1.0 + mod_ref[0, 4:5, :]) + mod_ref[0, 3:4, :]).astype(BF16)
        acc_ref[...] = jnp.zeros(acc_ref.shape, F32)

    h = h_ref[...]
    act = (jax.nn.silu(_dot(h, wg_ref[...])) * _dot(h, wu_ref[...])).astype(BF16)
    acc_ref[...] += _dot(act, wd_ref[...])

    @pl.when(f == nf - 1)
    def _():
        r = DN_ALPHA * x_ref[0] + mod_ref[0, 5:6, :] * acc_ref[...]
        o_ref[0] = _layer_norm(r, lng_ref[...], lnb_ref[...])


def _ffn(x, mod, w_gu, w_down, ln_g, ln_b, layer, idx):
    bm, lm, _ = x.shape
    tm, tf = 1024, 256
    nf = D_FF // tf
    return pl.pallas_call(
        functools.partial(_ffn_kernel, nf=nf),
        grid=(bm, lm // tm, nf),
        in_specs=[
            pl.BlockSpec((1, tm, D_MODEL), lambda b, i, f: (b, i, 0)),
            pl.BlockSpec((1, 6, D_MODEL), lambda b, i, f: (b, 0, 0)),
            pl.BlockSpec((None, D_MODEL, tf), lambda b, i, f: (idx, 0, f)),
            pl.BlockSpec((None, D_MODEL, tf), lambda b, i, f: (idx, 0, nf + f)),
            pl.BlockSpec((None, tf, D_MODEL), lambda b, i, f: (idx, f, 0)),
            pl.BlockSpec((None, None, 1, D_MODEL), lambda b, i, f: (layer, 1, 0, 0)),
            pl.BlockSpec((None, None, 1, D_MODEL), lambda b, i, f: (layer, 1, 0, 0)),
        ],
        out_specs=pl.BlockSpec((1, tm, D_MODEL), lambda b, i, f: (b, i, 0)),
        out_shape=jax.ShapeDtypeStruct((bm, lm, D_MODEL), F32),
        scratch_shapes=[pltpu.VMEM((tm, D_MODEL), BF16), pltpu.VMEM((tm, D_MODEL), F32)],
        compiler_params=_cparams(("parallel", "parallel", "arbitrary")),
        name="ffn",
    )(x, mod, w_gu, w_gu, w_down, ln_g, ln_b)


def _route(h, wr, br):
    logits = _dot_f32(h, wr)
    lane = lax.broadcasted_iota(jnp.int32, logits.shape, 1)
    valid = lane < N_EXPERTS
    neg = -jnp.inf
    sel = jnp.where(valid, logits + br, neg)
    m1 = jnp.max(sel, -1, keepdims=True)
    i1 = jnp.min(jnp.where(sel == m1, lane, LANES), -1, keepdims=True)
    sel2 = jnp.where(lane == i1, neg, sel)
    m2 = jnp.max(sel2, -1, keepdims=True)
    i2 = jnp.min(jnp.where(sel2 == m2, lane, LANES), -1, keepdims=True)
    l1 = jnp.sum(jnp.where(lane == i1, logits, 0.0), -1, keepdims=True)
    l2 = jnp.sum(jnp.where(lane == i2, logits, 0.0), -1, keepdims=True)
    mx = jnp.maximum(l1, l2)
    e1 = jnp.exp(l1 - mx)
    e2 = jnp.exp(l2 - mx)
    den = e1 + e2
    comb = jnp.where(lane == i1, e1 / den, 0.0) + jnp.where(lane == i2, e2 / den, 0.0)
    return comb, (lane == i1) | (lane == i2)


MOE_TM = 1024
MOE_CAP0 = 320
MOE_CAP1 = 256


def _moe_kernel(x_ref, mod_ref, wr_ref, br_ref, tri_ref, wg_ref, wu_ref, wd_ref, lng_ref, lnb_ref, o_ref,
                h_ref, comb_ref, slot_ref, slot_t_ref, acc_ref):
    e = pl.program_id(2)
    tm = h_ref.shape[0]

    @pl.when(e == 0)
    def _():
        hf = x_ref[0] * (1.0 + mod_ref[0, 4:5, :]) + mod_ref[0, 3:4, :]
        h_ref[...] = hf.astype(BF16)
        comb, sel = _route(hf, wr_ref[...], br_ref[...])
        comb_ref[...] = comb
        sel16 = jnp.where(sel, 1.0, 0.0).astype(BF16)
        nb = tri_ref.shape[0]
        carry = jnp.zeros((1, LANES), F32)
        for blk in range(tm // nb):
            rows = slice(blk * nb, (blk + 1) * nb)
            rank = _dot(tri_ref[...], sel16[rows]) + carry
            slot_ref[rows, :] = jnp.where(sel[rows], rank, -1.0)
            carry = carry + jnp.sum(sel16[rows].astype(F32), 0, keepdims=True)
        slot_t_ref[...] = slot_ref[...].T
        acc_ref[...] = jnp.zeros(acc_ref.shape, F32)

    lane = lax.broadcasted_iota(jnp.int32, comb_ref.shape, 1)
    pick = lane == e
    c_col = jnp.sum(jnp.where(pick, comb_ref[...], 0.0), -1, keepdims=True)
    slot_col = jnp.max(jnp.where(pick, slot_ref[...], -1.0), -1, keepdims=True)
    slot_row = slot_t_ref[pl.ds(e, 1), :]
    load = jnp.max(slot_row) + 1.0

    def block(base, cap):
        r_i = lax.broadcasted_iota(jnp.int32, (cap, tm), 0).astype(F32) + base
        gather = jnp.where(r_i == slot_row, 1.0, 0.0).astype(BF16)
        xe = _dot(gather, h_ref[...]).astype(BF16)
        act = (jax.nn.silu(_dot(xe, wg_ref[...])) * _dot(xe, wu_ref[...])).astype(BF16)
        ye = _dot(act, wd_ref[...]).astype(BF16)
        c_i = lax.broadcasted_iota(jnp.int32, (tm, cap), 1).astype(F32) + base
        scatter = jnp.where(c_i == slot_col, 1.0, 0.0).astype(BF16)
        acc_ref[...] += c_col * _dot(scatter, ye)

    block(0.0, MOE_CAP0)
    base = MOE_CAP0
    while base < tm:
        pl.when(load > base)(functools.partial(block, float(base), MOE_CAP1))
        base += MOE_CAP1

    @pl.when(e == N_EXPERTS - 1)
    def _():
        r = DN_ALPHA * x_ref[0] + mod_ref[0, 5:6, :] * acc_ref[...]
        o_ref[0] = _layer_norm(r, lng_ref[...], lnb_ref[...])


def _moe(x, mod, w_router_p, b_router_p, tri, w_gu, w_down, ln_g, ln_b, layer, idx):
    bm, lm, _ = x.shape
    tm = MOE_TM
    nb = tri.shape[0]
    return pl.pallas_call(
        _moe_kernel,
        grid=(bm, lm // tm, N_EXPERTS),
        in_specs=[
            pl.BlockSpec((1, tm, D_MODEL), lambda b, i, e: (b, i, 0)),
            pl.BlockSpec((1, 6, D_MODEL), lambda b, i, e: (b, 0, 0)),
            pl.BlockSpec((None, D_MODEL, LANES), lambda b, i, e: (idx, 0, 0)),
            pl.BlockSpec((None, 1, LANES), lambda b, i, e: (idx, 0, 0)),
            pl.BlockSpec((nb, nb), lambda b, i, e: (0, 0)),
            pl.BlockSpec((None, None, D_MODEL, D_FF_EXPERT), lambda b, i, e: (idx, e, 0, 0)),
            pl.BlockSpec((None, None, D_MODEL, D_FF_EXPERT), lambda b, i, e: (idx, e, 0, 1)),
            pl.BlockSpec((None, None, D_FF_EXPERT, D_MODEL), lambda b, i, e: (idx, e, 0, 0)),
            pl.BlockSpec((None, None, 1, D_MODEL), lambda b, i, e: (layer, 1, 0, 0)),
            pl.BlockSpec((None, None, 1, D_MODEL), lambda b, i, e: (layer, 1, 0, 0)),
        ],
        out_specs=pl.BlockSpec((1, tm, D_MODEL), lambda b, i, e: (b, i, 0)),
        out_shape=jax.ShapeDtypeStruct((bm, lm, D_MODEL), F32),
        scratch_shapes=[pltpu.VMEM((tm, D_MODEL), BF16), pltpu.VMEM((tm, LANES), F32),
                        pltpu.VMEM((tm, LANES), F32), pltpu.VMEM((LANES, tm), F32),
                        pltpu.VMEM((tm, D_MODEL), F32)],
        compiler_params=_cparams(("parallel", "parallel", "arbitrary")),
        name="moe",
    )(x, mod, w_router_p, b_router_p, tri, w_gu, w_gu, w_down, ln_g, ln_b)


def _pack_w_in(w_in):
    d = w_in.shape[0]
    o = 0
    cols = {}
    for name, width in (("a_u", A_WIDTH), ("a_v", A_WIDTH), ("dq", DN_QK), ("dk", DN_QK), ("dv", DN_VW),
                        ("dz", DN_VW), ("sm", 4 * DN_HEADS), ("cq", Q_LORA), ("ckv", KV_LORA), ("kr", MLA_ROPE),
                        ("ga", D_MODEL), ("gb", D_MODEL), ("gc", D_MODEL)):
        cols[name] = w_in[:, :, o:o + width]
        o += width
    zpad = lambda n: jnp.zeros((d, D_MODEL, n), w_in.dtype)
    packed = jnp.concatenate(
        [cols["ga"], cols["gb"], cols["gc"], cols["dq"], cols["dk"], cols["dv"], cols["dz"],
         cols["a_u"], cols["a_v"], cols["cq"], cols["ckv"],
         cols["kr"], zpad(LANES - MLA_ROPE), cols["sm"], zpad(LANES - 4 * DN_HEADS)], axis=-1)
    return packed.astype(BF16)


def _rope_tables(n_tok):
    row = (jnp.arange(n_tok) // GRID_W).astype(F32)
    col = (jnp.arange(n_tok) % GRID_W).astype(F32)
    inv_freq = ROPE_BASE ** (-jnp.arange(AXIS_FREQS, dtype=F32) / AXIS_FREQS)
    ang_r = row[:, None] * inv_freq
    ang_c = col[:, None] * inv_freq
    ones = jnp.ones((n_tok, LANES - MLA_ROPE), F32)
    cos = jnp.concatenate([jnp.cos(ang_r), jnp.cos(ang_r), jnp.cos(ang_c), jnp.cos(ang_c), ones], axis=1)
    sin = jnp.concatenate([-jnp.sin(ang_r), jnp.sin(ang_r), -jnp.sin(ang_c), jnp.sin(ang_c), 0.0 * ones], axis=1)
    return cos, sin


def kernel(x_prompt, x_sample, state_dn, cache_ckv, cache_krope, c, c_ctx, w_mod, b_mod, w_in, a_ln_g, a_ln_b, a_ws, a_bs, dn_conv, dn_a_log, dn_dt_bias, dn_norm, q_norm, w_qb, kv_norm, w_kvb, w_pa, w_pb, w_pc, w_out, ln_g, ln_b, ffn_gu, ffn_down, moe_router, moe_bias, moe_gu, moe_down):
    batch, seq, _ = x_prompt.shape
    dec_batch, dec_seq, _ = x_sample.shape
    depth = w_in.shape[0]
    assert depth == DEPTH

    w_in_p = _pack_w_in(w_in)
    a_bs_t = jnp.swapaxes(a_bs, 1, 2)
    ln_a_g = a_ln_g.reshape(depth, 1, A_WIDTH)
    ln_a_b = a_ln_b.reshape(depth, 1, A_WIDTH)
    lane_pad = lambda v: jnp.pad(v.reshape(depth, 1, -1), ((0, 0), (0, 0), (0, LANES - v[0].size)))
    alog_row = lane_pad(dn_a_log)
    dtb_row = lane_pad(dn_dt_bias)
    dn_g = dn_norm.reshape(depth, 1, DN_DV)
    qg = q_norm.reshape(depth, 1, Q_LORA)
    kvg = kv_norm.reshape(depth, 1, KV_LORA)
    wq = w_qb.reshape(depth, Q_LORA, MLA_HEADS, MLA_NOPE + MLA_ROPE)
    wqn = wq[..., :MLA_NOPE].reshape(depth, Q_LORA, MLA_HEADS * MLA_NOPE).astype(BF16)
    wqr = jnp.pad(wq[..., MLA_NOPE:], ((0, 0), (0, 0), (0, 0), (0, LANES - MLA_ROPE)))
    wqr = wqr.reshape(depth, Q_LORA, MLA_HEADS * LANES).astype(BF16)
    wkv = w_kvb.reshape(depth, KV_LORA, MLA_HEADS, MLA_NOPE + MLA_V)
    wkn = wkv[..., :MLA_NOPE].reshape(depth, KV_LORA, MLA_HEADS * MLA_NOPE).astype(BF16)
    wv = wkv[..., MLA_NOPE:].reshape(depth, KV_LORA, MLA_HEADS * MLA_V).astype(BF16)
    w_pa16, w_pb16, w_pc16, w_out16 = (w.astype(BF16) for w in (w_pa, w_pb, w_pc, w_out))
    ffn_gu16, ffn_down16 = ffn_gu.astype(BF16), ffn_down.astype(BF16)
    moe_gu16, moe_down16 = moe_gu.astype(BF16), moe_down.astype(BF16)
    n_moe = moe_router.shape[0]
    wr_p = jnp.pad(moe_router, ((0, 0), (0, 0), (0, LANES - N_EXPERTS)))
    br_p = jnp.pad(moe_bias.reshape(n_moe, 1, N_EXPERTS), ((0, 0), (0, 0), (0, LANES - N_EXPERTS)))
    ln_g4 = ln_g.reshape(depth, 2, 1, D_MODEL)
    ln_b4 = ln_b.reshape(depth, 2, 1, D_MODEL)
    rope_tabs = _rope_tables(dec_seq)
    gdn_tabs = _gdn_tables(GDN_TILE)
    moe_tri = (jnp.arange(256)[None, :] < jnp.arange(256)[:, None]).astype(BF16)

    n_cond = 16
    cond = jnp.concatenate([c, c_ctx[None, :], jnp.zeros((n_cond - dec_batch - 1, D_MODEL), F32)], axis=0)
    mods = _adaln(cond, w_mod, b_mod).reshape(depth, n_cond, 6, D_MODEL)

    xp = x_prompt.reshape(1, batch * seq, D_MODEL)
    xs = x_sample
    st_dn, st_ckv, st_kr = [], [], []

    def channel_mixer(x, mod, l):
        if l % 2 == 0:
            return _ffn(x, mod, ffn_gu16, ffn_down16, ln_g4, ln_b4, l, l // 2)
        return _moe(x, mod, wr_p, br_p, moe_tri, moe_gu16, moe_down16, ln_g4, ln_b4, l, l // 2)

    def token_mixer(x, mod, l, bsz, n_tok, latent):
        y = _inproj(x, mod, w_in_p, l)
        o_a = _gmlp(y.reshape(-1, NP), a_ws, a_bs_t, ln_a_g, ln_a_b, l).reshape(x.shape[0], x.shape[1], A_WIDTH)
        ys = y.reshape(bsz, n_tok, NP)
        o_f, o_b, s_fin = _gdn(ys, dn_conv, alog_row, dtb_row, gdn_tabs, state_dn if latent else None, l)
        if latent:
            q, k, v = _mla_prep(ys, qg, kvg, wqn, wqr, wkn, wv, l, rope_tabs, False)
            ctx_kv = _mla_ctx(cache_ckv, cache_krope, wkn, wv, l)
            o_c = _attention(q, k, v, ctx_kv, 512, 512)
            extra = None
        else:
            q, k, v, ckv_n, kr = _mla_prep(ys, qg, kvg, wqn, wqr, wkn, wv, l, None, True)
            o_c = _attention(q, k, v, None, n_tok, n_tok)
            extra = (s_fin, ckv_n, kr)
        shp = (x.shape[0], x.shape[1], 512)
        x = _merge(o_a, o_f.reshape(shp), o_b.reshape(shp), y, o_c.reshape(shp), x, mod, dn_g,
                   w_pa16, w_pb16, w_pc16, w_out16, ln_g4, ln_b4, l)
        return x, extra

    for l in range(depth):
        mod_p = mods[l, dec_batch:dec_batch + 1]
        mod_s = mods[l, :dec_batch]
        xp, (s_fin, ckv_n, kr) = token_mixer(xp, mod_p, l, batch, seq, False)
        xp = channel_mixer(xp, mod_p, l)
        st_dn.append(s_fin)
        st_ckv.append(ckv_n)
        st_kr.append(kr)
        xs, _ = token_mixer(xs, mod_s, l, dec_batch, dec_seq, True)
        xs = channel_mixer(xs, mod_s, l)

    return (xp.reshape(batch, seq, D_MODEL), xs,
            jnp.stack(st_dn, axis=1), jnp.stack(st_ckv, axis=1), jnp.stack(st_kr, axis=1))
```

```python
import functools
import math

import jax
import jax.numpy as jnp
from jax import lax
from jax.experimental import pallas as pl
from jax.experimental.pallas import tpu as pltpu

F32 = jnp.float32
BF16 = jnp.bfloat16

D_MODEL = 1024
DEPTH = 4
GRID_W = 64
A_WIDTH = 512
A_CHUNK = 128
A_GROUPS = 4
DN_HEADS = 4
DN_DK = 128
DN_DV = 128
DN_QK = DN_HEADS * DN_DK
DN_VW = DN_HEADS * DN_DV
DN_CONV_K = 5
GDN_TILE = 256
MLA_HEADS = 4
MLA_NOPE = 128
MLA_ROPE = 64
MLA_V = 128
Q_LORA = 256
KV_LORA = 256
AXIS_FREQS = MLA_ROPE // 4
ROPE_BASE = 10000.0
D_FF = 2816
N_EXPERTS = 8
D_FF_EXPERT = D_FF // 2
DN_ALPHA = (2 * DEPTH) ** 0.25
EPS = 1e-6
LN_EPS = 1e-5

LANES = 128
HALO = 16
VMEM_LIMIT = 56 * 1024 * 1024

GATE0 = 0
QKV0 = 3 * D_MODEL
DZ0 = QKV0 + 3 * DN_QK
AU0 = DZ0 + DN_VW
AV0 = AU0 + A_WIDTH
CQ0 = AV0 + A_WIDTH
CKV0 = CQ0 + Q_LORA
KR0 = CKV0 + KV_LORA
SM0 = KR0 + LANES
NP = SM0 + LANES
QKV_W = 3 * DN_QK


def _cparams(sem):
    return pltpu.CompilerParams(dimension_semantics=sem, vmem_limit_bytes=VMEM_LIMIT)


def _dot(a, b):
    return jnp.dot(a, b, preferred_element_type=F32)


def _dot_nt(a, b):
    return lax.dot_general(a, b, (((1,), (1,)), ((), ())), preferred_element_type=F32)


def _dot_tn(a, b):
    return lax.dot_general(a, b, (((0,), (0,)), ((), ())), preferred_element_type=F32)


def _split3(x):
    hi = x.astype(BF16)
    r = x - hi.astype(F32)
    mid = r.astype(BF16)
    lo = (r - mid.astype(F32)).astype(BF16)
    return hi, mid, lo


def _dot_f32(a, b):
    ah, am, al = _split3(a)
    bh, bm, bl = _split3(b)
    return (_dot(ah, bh) + (_dot(ah, bm) + _dot(am, bh))
            + (_dot(am, bm) + _dot(ah, bl) + _dot(al, bh)))


def _layer_norm(r, g, b):
    mu = jnp.mean(r, -1, keepdims=True)
    rc = r - mu
    var = jnp.mean(rc * rc, -1, keepdims=True)
    return rc * lax.rsqrt(var + LN_EPS) * g + b


def _adaln_kernel(c_ref, w_ref, b_ref, o_ref):
    s = jax.nn.silu(c_ref[...])
    o_ref[...] = _dot_f32(s, w_ref[...]) + b_ref[...]


def _adaln(cond, w_mod, b_mod):
    rows = cond.shape[0]
    n_out = w_mod.shape[-1]
    tn = 1536
    return pl.pallas_call(
        _adaln_kernel,
        grid=(DEPTH, n_out // tn),
        in_specs=[
            pl.BlockSpec((rows, D_MODEL), lambda l, j: (0, 0)),
            pl.BlockSpec((None, D_MODEL, tn), lambda l, j: (l, 0, j)),
            pl.BlockSpec((None, 1, tn), lambda l, j: (l, 0, j)),
        ],
        out_specs=pl.BlockSpec((None, rows, tn), lambda l, j: (l, 0, j)),
        out_shape=jax.ShapeDtypeStruct((DEPTH, rows, n_out), F32),
        compiler_params=_cparams(("parallel", "parallel")),
        name="adaln",
    )(cond, w_mod, b_mod.reshape(DEPTH, 1, n_out))


def _inproj_kernel(x_ref, mod_ref, w_ref, y_ref, h_ref):
    @pl.when(pl.program_id(2) == 0)
    def _():
        sh = mod_ref[0, 0:1, :]
        sc = mod_ref[0, 1:2, :]
        h_ref[...] = (x_ref[0] * (1.0 + sc) + sh).astype(BF16)

    y_ref[0] = _dot(h_ref[...], w_ref[...]).astype(y_ref.dtype)


def _inproj(x, mod, w_in_p, layer):
    bm, lm, _ = x.shape
    tm, tn = 1024, 1152
    return pl.pallas_call(
        _inproj_kernel,
        grid=(bm, lm // tm, NP // tn),
        in_specs=[
            pl.BlockSpec((1, tm, D_MODEL), lambda b, i, j: (b, i, 0)),
            pl.BlockSpec((1, 6, D_MODEL), lambda b, i, j: (b, 0, 0)),
            pl.BlockSpec((None, D_MODEL, tn), lambda b, i, j: (layer, 0, j)),
        ],
        out_specs=pl.BlockSpec((1, tm, tn), lambda b, i, j: (b, i, j)),
        out_shape=jax.ShapeDtypeStruct((bm, lm, NP), BF16),
        scratch_shapes=[pltpu.VMEM((tm, D_MODEL), BF16)],
        compiler_params=_cparams(("parallel", "parallel", "arbitrary")),
        name="inproj",
    )(x, mod, w_in_p)


def _gmlp_kernel(u_ref, v_ref, ws_ref, bs_ref, g_ref, b_ref, o_ref):
    tl = u_ref.shape[0]
    gd = A_WIDTH // A_GROUPS
    v = jax.nn.gelu(v_ref[...].astype(F32))
    vn = _layer_norm(v, g_ref[...], b_ref[...]).astype(BF16)
    u = jax.nn.gelu(u_ref[...].astype(F32))
    for n in range(tl // A_CHUNK):
        rows = slice(n * A_CHUNK, (n + 1) * A_CHUNK)
        for g in range(A_GROUPS):
            cols = slice(g * gd, (g + 1) * gd)
            mixed = _dot(ws_ref[g].astype(BF16), vn[rows, cols]) + bs_ref[:, g:g + 1]
            o_ref[rows, cols] = (u[rows, cols] * mixed).astype(o_ref.dtype)


def _gmlp(y2d, a_ws, a_bs_t, ln_g, ln_b, layer):
    t = y2d.shape[0]
    tl = 512
    return pl.pallas_call(
        _gmlp_kernel,
        grid=(t // tl,),
        in_specs=[
            pl.BlockSpec((tl, A_WIDTH), lambda i: (i, AU0 // A_WIDTH)),
            pl.BlockSpec((tl, A_WIDTH), lambda i: (i, AV0 // A_WIDTH)),
            pl.BlockSpec((None, A_GROUPS, A_CHUNK, A_CHUNK), lambda i: (layer, 0, 0, 0)),
            pl.BlockSpec((None, A_CHUNK, A_GROUPS), lambda i: (layer, 0, 0)),
            pl.BlockSpec((None, 1, A_WIDTH), lambda i: (layer, 0, 0)),
            pl.BlockSpec((None, 1, A_WIDTH), lambda i: (layer, 0, 0)),
        ],
        out_specs=pl.BlockSpec((tl, A_WIDTH), lambda i: (i, 0)),
        out_shape=jax.ShapeDtypeStruct((t, A_WIDTH), BF16),
        compiler_params=_cparams(("parallel",)),
        name="gmlp",
    )(y2d, y2d, a_ws, a_bs_t, ln_g, ln_b)


GDN_SCRATCH = {"kb": DN_DK, "k": DN_DK, "q": DN_DK, "qe": DN_DK, "kdec": DN_DK, "rhs": DN_DV + DN_DK,
               "low": None, "qk": None, "inv": None}


def _gdn_stage1(d, reverse, is_first, is_last, qkv_ref, prev_ref, next_ref, sm_ref, conv_ref, alog_ref,
                dtb_ref, tri_ref, lvl_ref, ext_ref, act_ref, tile, conv_now, o_ref, tl, sc, probs):
    c = tl
    off = HALO - DN_CONV_K // 2
    st = {}

    def fill():
        ext_ref[0:HALO, :] = jnp.where(is_first, 0.0, prev_ref[0].astype(F32))
        ext_ref[HALO:HALO + tl, :] = qkv_ref[0].astype(F32)
        ext_ref[HALO + tl:2 * HALO + tl, :] = jnp.where(is_last, 0.0, next_ref[0].astype(F32))

    rb = 32

    def conv(part, r0):
        cols = slice(part * DN_QK, (part + 1) * DN_QK)
        acc = conv_ref[0:1, cols] * ext_ref[off + r0:off + r0 + rb, cols]
        for j in range(1, DN_CONV_K):
            acc = acc + conv_ref[j:j + 1, cols] * ext_ref[off + j + r0:off + j + r0 + rb, cols]
        y = jax.nn.silu(acc)
        if part < 2:
            for h in range(DN_HEADS):
                yh = y[:, h * DN_DK:(h + 1) * DN_DK]
                yh = yh * lax.rsqrt(jnp.sum(yh * yh, -1, keepdims=True) + EPS)
                act_ref[tile, r0:r0 + rb, part * DN_QK + h * DN_DK:part * DN_QK + (h + 1) * DN_DK] = (
                    yh.astype(act_ref.dtype))
        else:
            act_ref[tile, r0:r0 + rb, cols] = y.astype(act_ref.dtype)

    def conv_all():
        fill()
        for part in range(3):
            for r0 in range(0, tl, rb):
                conv(part, r0)

    def maybe_conv():
        if conv_now is not None:
            pl.when(conv_now)(conv_all)

    def gates():
        sm = sm_ref[0].astype(F32)
        g_all = -jnp.exp(alog_ref[...]) * jax.nn.softplus(sm + dtb_ref[...])
        st["beta"] = jax.nn.sigmoid(sm)
        tri = tri_ref[d]
        gh, gm, gl = _split3(g_all)
        st["gam"] = _dot(tri, gh) + _dot(tri, gm) + _dot(tri, gl)
        st["gam_t"] = st["gam"].T
        ii = lax.broadcasted_iota(jnp.int32, (c, c), 0)
        jj = lax.broadcasted_iota(jnp.int32, (c, c), 1)
        st["incl"] = jnp.where((jj >= ii) if reverse else (jj <= ii), 1.0, 0.0).astype(F32)
        st["strict"] = jnp.where((jj > ii) if reverse else (jj < ii), 1.0, 0.0).astype(F32)
        st["eye"] = jnp.where(ii == jj, 1.0, 0.0).astype(F32)

    def head(h):
        scale = DN_DK ** -0.5
        lane = d * DN_HEADS + h
        pi = lane
        gam, gam_t, beta = st["gam"], st["gam_t"], st["beta"]
        gr = gam_t[lane:lane + 1, :]
        g_last = gam[0:1, lane:lane + 1] if reverse else gam[c - 1:c, lane:lane + 1]
        ob = 64
        for r0 in range(0, c, ob):
            rows = slice(r0, r0 + ob)
            gc = gam[rows, lane:lane + 1]
            bcol = beta[rows, 2 * DN_HEADS + lane:2 * DN_HEADS + lane + 1]
            eg = jnp.exp(gc)
            qh = act_ref[tile, rows, h * DN_DK:(h + 1) * DN_DK].astype(F32) * scale
            kh = act_ref[tile, rows, DN_QK + h * DN_DK:DN_QK + (h + 1) * DN_DK].astype(F32)
            vh = act_ref[tile, rows, 2 * DN_QK + h * DN_DV:2 * DN_QK + (h + 1) * DN_DV].astype(F32)
            kb = kh * bcol
            sc["kb"][pi, rows, :] = kb.astype(BF16)
            sc["k"][pi, rows, :] = kh.astype(BF16)
            sc["q"][pi, rows, :] = qh.astype(BF16)
            sc["qe"][pi, rows, :] = (qh * eg).astype(BF16)
            sc["kdec"][pi, rows, :] = (kh * jnp.exp(g_last - gc)).astype(BF16)
            sc["rhs"][pi, rows, 0:DN_DV] = (vh * bcol).astype(BF16)
            sc["rhs"][pi, rows, DN_DV:2 * DN_DV] = (kb * eg).astype(BF16)
        kk = _dot_nt(sc["kb"][pi], sc["k"][pi])
        qk = _dot_nt(sc["q"][pi], sc["k"][pi])
        mb = 32
        for r0 in range(0, c, mb):
            rows = slice(r0, r0 + mb)
            gc = gam[rows, lane:lane + 1]
            dec = jnp.exp(jnp.minimum(gc - gr, 0.0))
            low = kk[rows] * dec * st["strict"][rows]
            sc["qk"][pi, rows, :] = (qk[rows] * dec * st["incl"][rows]).astype(BF16)
            sc["low"][pi, rows, :] = low.astype(BF16)
            sc["inv"][pi, rows, :] = (st["eye"][rows] - low * lvl_ref[d, 0, rows, :]).astype(BF16)
        probs.append(dict(d=d, h=h, pi=pi, o_ref=o_ref, e_last=jnp.exp(g_last)))

    return [maybe_conv, gates] + [functools.partial(head, h) for h in range(DN_HEADS)]


def _gdn_level(probs, lvl_ref, sc, lv):
    for p in probs:
        p["t16"] = _dot(sc["low"][p["pi"]], sc["inv"][p["pi"]]).astype(BF16)
    for p in probs:
        x16 = _dot(sc["inv"][p["pi"]], p["t16"]).astype(BF16)
        sc["inv"][p["pi"]] = sc["inv"][p["pi"]] - x16 * lvl_ref[p["d"], lv]


def _gdn_solve(p, sc):
    sol = _dot(sc["inv"][p["pi"]], sc["rhs"][p["pi"]])
    p["u"] = sol[:, :DN_DV]
    p["w16"] = sol[:, DN_DV:].astype(BF16)


def _gdn_scan(p, s_ref, sc):
    st = s_ref[p["d"], p["h"]]
    st16 = st.astype(BF16)
    v16 = (p["u"] - _dot(p["w16"], st16)).astype(BF16)
    o = _dot(sc["qe"][p["pi"]], st16) + _dot(sc["qk"][p["pi"]], v16)
    s_ref[p["d"], p["h"]] = st * p["e_last"] + _dot_tn(sc["kdec"][p["pi"]], v16)
    p["o_ref"][0, :, p["h"] * DN_DV:(p["h"] + 1) * DN_DV] = o.astype(p["o_ref"].dtype)


def _gdn_kernel(*refs, tl, n_t, has_init):
    (qkv_f, prev_f, next_f, sm_f, qkv_b, prev_b, next_b, sm_b,
     conv_ref, alog_ref, dtb_ref, tri_ref, lvl_ref) = refs[:13]
    k = 13
    if has_init:
        s0_ref = refs[k]
        k += 1
    o_f, o_b, sfin_ref, ext_f, ext_b, act_c, s_ref = refs[k:k + 7]
    sc = dict(zip(GDN_SCRATCH, refs[k + 7:]))
    i = pl.program_id(1)
    n_lvl = lvl_ref.shape[1]
    first_visit = i < (n_t + 1) // 2
    conv_b = None if n_t == 1 else first_visit

    @pl.when(i == 0)
    def _():
        if has_init:
            s_ref[...] = s0_ref[0]
        else:
            s_ref[...] = jnp.zeros(s_ref.shape, F32)

    probs_f, probs_b = [], []
    items_f = _gdn_stage1(0, False, i == 0, i == n_t - 1, qkv_f, prev_f, next_f, sm_f, conv_ref, alog_ref,
                          dtb_ref, tri_ref, lvl_ref, ext_f, act_c, i, first_visit, o_f, tl, sc, probs_f)
    items_b = _gdn_stage1(1, True, i == n_t - 1, i == 0, qkv_b, prev_b, next_b, sm_b, conv_ref, alog_ref,
                          dtb_ref, tri_ref, lvl_ref, ext_b, act_c, n_t - 1 - i, conv_b, o_b, tl, sc, probs_b)
    for item in items_f + items_b:
        item()
    probs = probs_f + probs_b
    for lv in range(1, n_lvl):
        _gdn_level(probs, lvl_ref, sc, lv)
    for p in probs:
        _gdn_solve(p, sc)
    for p in probs:
        _gdn_scan(p, s_ref, sc)

    @pl.when(i == n_t - 1)
    def _():
        sfin_ref[0] = s_ref[...]


def _gdn_tables(c):
    ii = jnp.arange(c)[:, None]
    jj = jnp.arange(c)[None, :]
    tri = jnp.stack([jj <= ii, jj >= ii]).astype(BF16)
    x = ii ^ jj
    lvls = []
    s = 1
    while s < c:
        pair = (x >= s) & (x < 2 * s)
        lvls.append(jnp.stack([pair & (jj < ii), pair & (jj > ii)]))
        s *= 2
    return tri, jnp.stack(lvls, axis=1).astype(BF16)


def _gdn(y, conv_w, alog_row, dtb_row, tables, s0, layer):
    b, l, _ = y.shape
    tl = GDN_TILE
    n_t = l // tl
    assert n_t == 1 or n_t % 2 == 0
    hb = tl // HALO
    n_hb = l // HALO
    qkv_blk = QKV0 // QKV_W
    sm_blk = SM0 // LANES
    has_init = s0 is not None

    def fwd(i):
        return i

    def bwd(i):
        return n_t - 1 - i

    def tile_specs(pos):
        return [
            pl.BlockSpec((1, tl, QKV_W), lambda bb, i: (bb, pos(i), qkv_blk)),
            pl.BlockSpec((1, HALO, QKV_W), lambda bb, i: (bb, jnp.maximum(pos(i) * hb - 1, 0), qkv_blk)),
            pl.BlockSpec((1, HALO, QKV_W), lambda bb, i: (bb, jnp.minimum((pos(i) + 1) * hb, n_hb - 1), qkv_blk)),
            pl.BlockSpec((1, tl, LANES), lambda bb, i: (bb, pos(i), sm_blk)),
        ]

    tri, lvls = tables
    n_lvl = lvls.shape[1]
    in_specs = tile_specs(fwd) + tile_specs(bwd) + [
        pl.BlockSpec((None, DN_CONV_K, QKV_W), lambda bb, i: (layer, 0, 0)),
        pl.BlockSpec((None, 1, LANES), lambda bb, i: (layer, 0, 0)),
        pl.BlockSpec((None, 1, LANES), lambda bb, i: (layer, 0, 0)),
        pl.BlockSpec((2, tl, tl), lambda bb, i: (0, 0, 0)),
        pl.BlockSpec((2, n_lvl, tl, tl), lambda bb, i: (0, 0, 0, 0)),
    ]
    args = [y] * 8 + [conv_w, alog_row, dtb_row, tri, lvls]
    if has_init:
        in_specs.append(pl.BlockSpec((1, None, 2, DN_HEADS, DN_DK, DN_DV), lambda bb, i: (bb, layer, 0, 0, 0, 0)))
        args.append(s0)
    return pl.pallas_call(
        functools.partial(_gdn_kernel, tl=tl, n_t=n_t, has_init=has_init),
        grid=(b, n_t),
        in_specs=in_specs,
        out_specs=[
            pl.BlockSpec((1, tl, DN_VW), lambda bb, i: (bb, i, 0)),
            pl.BlockSpec((1, tl, DN_VW), lambda bb, i: (bb, n_t - 1 - i, 0)),
            pl.BlockSpec((1, 2, DN_HEADS, DN_DK, DN_DV), lambda bb, i: (bb, 0, 0, 0, 0)),
        ],
        out_shape=[
            jax.ShapeDtypeStruct((b, l, DN_VW), BF16),
            jax.ShapeDtypeStruct((b, l, DN_VW), BF16),
            jax.ShapeDtypeStruct((b, 2, DN_HEADS, DN_DK, DN_DV), F32),
        ],
        scratch_shapes=[
            pltpu.VMEM((tl + 2 * HALO, QKV_W), F32),
            pltpu.VMEM((tl + 2 * HALO, QKV_W), F32),
            pltpu.VMEM((n_t, tl, QKV_W), BF16),
            pltpu.VMEM((2, DN_HEADS, DN_DK, DN_DV), F32),
        ] + [pltpu.VMEM((2 * DN_HEADS, tl, w or tl), BF16) for w in GDN_SCRATCH.values()],
        compiler_params=_cparams(("parallel", "arbitrary")),
        name="gdn",
    )(*args)


def _rms(x, g):
    return x * lax.rsqrt(jnp.mean(x * x, -1, keepdims=True) + EPS) * g


def _rope128(x, cos, sin):
    lane = lax.broadcasted_iota(jnp.int32, x.shape, 1)
    first = (lane % (2 * AXIS_FREQS)) < AXIS_FREQS
    sw = jnp.where(first, pltpu.roll(x, LANES - AXIS_FREQS, 1), pltpu.roll(x, AXIS_FREQS, 1))
    return x * cos + sw * sin


def _mla_prep_kernel(*refs, rope, emit_cache):
    cq_ref, ckv_ref, kr_ref, qg_ref, kvg_ref, wqn_ref, wqr_ref, wkn_ref, wv_ref = refs[:9]
    k = 9
    if rope:
        cos_ref, sin_ref = refs[k:k + 2]
        k += 2
    q_out, k_out, v_out = refs[k:k + 3]
    k += 3
    scale = (MLA_NOPE + MLA_ROPE) ** -0.5
    cqn = _rms(cq_ref[0].astype(F32), qg_ref[...]).astype(BF16)
    qn = _dot(cqn, wqn_ref[...]) * scale
    qr = _dot(cqn, wqr_ref[...]) * scale
    ckvn = _rms(ckv_ref[0].astype(F32), kvg_ref[...])
    ckvn16 = ckvn.astype(BF16)
    kn = _dot(ckvn16, wkn_ref[...])
    v_out[0] = _dot(ckvn16, wv_ref[...]).T.astype(v_out.dtype)
    kr = kr_ref[0].astype(F32)
    if emit_cache:
        ckvn_out, kr_out = refs[k:k + 2]
        ckvn_out[0] = ckvn
        kr_out[0] = kr[:, :MLA_ROPE]
    if rope:
        cos = cos_ref[...]
        sin = sin_ref[...]
        kr = _rope128(kr, cos, sin)
    kr16 = kr.astype(k_out.dtype)
    for h in range(MLA_HEADS):
        qrh = qr[:, h * LANES:(h + 1) * LANES]
        if rope:
            qrh = _rope128(qrh, cos, sin)
        q_out[0, :, h * 256:h * 256 + 128] = qn[:, h * 128:(h + 1) * 128].astype(q_out.dtype)
        q_out[0, :, h * 256 + 128:(h + 1) * 256] = qrh.astype(q_out.dtype)
        k_out[0, :, h * 256:h * 256 + 128] = kn[:, h * 128:(h + 1) * 128].astype(k_out.dtype)
        k_out[0, :, h * 256 + 128:(h + 1) * 256] = kr16


def _mla_prep(y, q_norm, kv_norm, wqn, wqr, wkn, wv, layer, rope_tabs, emit_cache):
    b, l, _ = y.shape
    tm = 256
    rope = rope_tabs is not None
    w_spec = lambda shp: pl.BlockSpec((None,) + shp, lambda bb, i: (layer, 0, 0))
    in_specs = [
        pl.BlockSpec((1, tm, Q_LORA), lambda bb, i: (bb, i, CQ0 // Q_LORA)),
        pl.BlockSpec((1, tm, KV_LORA), lambda bb, i: (bb, i, CKV0 // KV_LORA)),
        pl.BlockSpec((1, tm, LANES), lambda bb, i: (bb, i, KR0 // LANES)),
        w_spec((1, Q_LORA)), w_spec((1, KV_LORA)),
        w_spec((Q_LORA, 512)), w_spec((Q_LORA, 512)), w_spec((KV_LORA, 512)), w_spec((KV_LORA, 512)),
    ]
    args = [y, y, y, q_norm, kv_norm, wqn, wqr, wkn, wv]
    if rope:
        in_specs += [pl.BlockSpec((tm, LANES), lambda bb, i: (i, 0))] * 2
        args += list(rope_tabs)
    out_specs = [
        pl.BlockSpec((1, tm, 1024), lambda bb, i: (bb, i, 0)),
        pl.BlockSpec((1, tm, 1024), lambda bb, i: (bb, i, 0)),
        pl.BlockSpec((1, 512, tm), lambda bb, i: (bb, 0, i)),
    ]
    out_shape = [
        jax.ShapeDtypeStruct((b, l, 1024), BF16),
        jax.ShapeDtypeStruct((b, l, 1024), BF16),
        jax.ShapeDtypeStruct((b, 512, l), BF16),
    ]
    if emit_cache:
        out_specs += [pl.BlockSpec((1, tm, KV_LORA), lambda bb, i: (bb, i, 0)),
                      pl.BlockSpec((1, tm, MLA_ROPE), lambda bb, i: (bb, i, 0))]
        out_shape += [jax.ShapeDtypeStruct((b, l, KV_LORA), F32),
                      jax.ShapeDtypeStruct((b, l, MLA_ROPE), F32)]
    return pl.pallas_call(
        functools.partial(_mla_prep_kernel, rope=rope, emit_cache=emit_cache),
        grid=(b, l // tm),
        in_specs=in_specs,
        out_specs=out_specs,
        out_shape=out_shape,
        compiler_params=_cparams(("parallel", "parallel")),
        name="mla_prep",
    )(*args)


def _mla_ctx_kernel(ckv_ref, kr_ref, wkn_ref, wv_ref, k_out, v_out):
    c16 = ckv_ref[0].astype(BF16)
    kn = _dot(c16, wkn_ref[...])
    v_out[0] = _dot(c16, wv_ref[...]).T.astype(v_out.dtype)
    kr16 = kr_ref[0].astype(k_out.dtype)
    zeros = jnp.zeros((kr16.shape[0], LANES - MLA_ROPE), k_out.dtype)
    for h in range(MLA_HEADS):
        k_out[0, :, h * 256:h * 256 + 128] = kn[:, h * 128:(h + 1) * 128].astype(k_out.dtype)
        k_out[0, :, h * 256 + 128:h * 256 + 128 + MLA_ROPE] = kr16
        k_out[0, :, h * 256 + 128 + MLA_ROPE:(h + 1) * 256] = zeros


def _mla_ctx(cache_ckv, cache_krope, wkn, wv, layer):
    b, _, p, _ = cache_ckv.shape
    return pl.pallas_call(
        _mla_ctx_kernel,
        grid=(b,),
        in_specs=[
            pl.BlockSpec((1, None, p, KV_LORA), lambda bb: (bb, layer, 0, 0)),
            pl.BlockSpec((1, None, p, MLA_ROPE), lambda bb: (bb, layer, 0, 0)),
            pl.BlockSpec((None, KV_LORA, 512), lambda bb: (layer, 0, 0)),
            pl.BlockSpec((None, KV_LORA, 512), lambda bb: (layer, 0, 0)),
        ],
        out_specs=[pl.BlockSpec((1, p, 1024), lambda bb: (bb, 0, 0)),
                   pl.BlockSpec((1, 512, p), lambda bb: (bb, 0, 0))],
        out_shape=[jax.ShapeDtypeStruct((b, p, 1024), BF16),
                   jax.ShapeDtypeStruct((b, 512, p), BF16)],
        compiler_params=_cparams(("parallel",)),
        name="mla_ctx",
    )(cache_ckv, cache_krope, wkn, wv)


def _attn_kernel(*refs, has_ctx, nk):
    q_ref = refs[0]
    if has_ctx:
        kc_ref, vc_ref, k_ref, v_ref, o_ref, m_ref, l_ref, acc_ref = refs[1:]
    else:
        k_ref, v_ref, o_ref, m_ref, l_ref, acc_ref = refs[1:]
    kk = pl.program_id(2)

    @pl.when(kk == 0)
    def _():
        m_ref[...] = jnp.full(m_ref.shape, -jnp.inf, F32)
        l_ref[...] = jnp.zeros(l_ref.shape, F32)
        acc_ref[...] = jnp.zeros(acc_ref.shape, F32)

    def body(kr, vtr):
        heads = range(MLA_HEADS)
        st = [_dot_nt(kr[0, :, h * 256:(h + 1) * 256], q_ref[0, :, h * 256:(h + 1) * 256]) for h in heads]
        m_prev = [m_ref[h] for h in heads]
        m_new = [jnp.maximum(m_prev[h], jnp.max(st[h], 0, keepdims=True)) for h in heads]
        alpha = [jnp.exp(m_prev[h] - m_new[h]) for h in heads]
        p = [jnp.exp(st[h] - m_new[h]) for h in heads]
        for h in heads:
            l_ref[h] = alpha[h] * l_ref[h] + jnp.sum(p[h], 0, keepdims=True)
            m_ref[h] = m_new[h]
        for h in heads:
            rows = slice(h * MLA_V, (h + 1) * MLA_V)
            acc_ref[rows, :] = alpha[h] * acc_ref[rows, :] + _dot(vtr[0, rows, :], p[h].astype(BF16))

    if has_ctx:
        @pl.when(kk == 0)
        def _():
            body(kc_ref, vc_ref)

        @pl.when(kk > 0)
        def _():
            body(k_ref, v_ref)
    else:
        body(k_ref, v_ref)

    @pl.when(kk == nk - 1)
    def _():
        for h in range(MLA_HEADS):
            hc = slice(h * MLA_V, (h + 1) * MLA_V)
            o_ref[0, :, hc] = (acc_ref[hc, :] / l_ref[h]).T.astype(o_ref.dtype)


def _attention(q, k, vt, ctx_kv, tq, tk):
    b, l, _ = q.shape
    has_ctx = ctx_kv is not None
    n_lat = l // tk
    nk = n_lat + (1 if has_ctx else 0)
    in_specs = [pl.BlockSpec((1, tq, 1024), lambda bb, i, j: (bb, i, 0))]
    args = [q]
    if has_ctx:
        assert ctx_kv[0].shape[1] == tk
        in_specs += [pl.BlockSpec((1, tk, 1024), lambda bb, i, j: (bb, 0, 0)),
                     pl.BlockSpec((1, 512, tk), lambda bb, i, j: (bb, 0, 0))]
        args += list(ctx_kv)
        kv_blk = lambda j: jnp.maximum(j - 1, 0)
    else:
        kv_blk = lambda j: j
    in_specs += [pl.BlockSpec((1, tk, 1024), lambda bb, i, j: (bb, kv_blk(j), 0)),
                 pl.BlockSpec((1, 512, tk), lambda bb, i, j: (bb, 0, kv_blk(j)))]
    args += [k, vt]
    return pl.pallas_call(
        functools.partial(_attn_kernel, has_ctx=has_ctx, nk=nk),
        grid=(b, l // tq, nk),
        in_specs=in_specs,
        out_specs=pl.BlockSpec((1, tq, 512), lambda bb, i, j: (bb, i, 0)),
        out_shape=jax.ShapeDtypeStruct((b, l, 512), BF16),
        scratch_shapes=[
            pltpu.VMEM((MLA_HEADS, 1, tq), F32),
            pltpu.VMEM((MLA_HEADS, 1, tq), F32),
            pltpu.VMEM((MLA_HEADS * MLA_V, tq), F32),
        ],
        compiler_params=_cparams(("parallel", "parallel", "arbitrary")),
        name="attn",
    )(*args)


def _merge_kernel(oa_ref, of_ref, ob_ref, dz_ref, oc_ref, gate_ref, x_ref, mod_ref, dng_ref,
                  wpa_ref, wpb_ref, wpc_ref, wout_ref, lng_ref, lnb_ref, o_ref):
    dng = dng_ref[...]
    s = of_ref[0].astype(F32) + ob_ref[0].astype(F32)
    dz = dz_ref[0].astype(F32)
    parts = []
    for h in range(DN_HEADS):
        hc = slice(h * DN_DV, (h + 1) * DN_DV)
        parts.append((_rms(s[:, hc], dng) * jax.nn.silu(dz[:, hc])).astype(BF16))
    o_dn = jnp.concatenate(parts, axis=1)
    gates = gate_ref[0]
    ga = jax.nn.sigmoid(gates[:, 0:D_MODEL].astype(F32))
    merged = ga * _dot(oa_ref[0], wpa_ref[...])
    gb = jax.nn.sigmoid(gates[:, D_MODEL:2 * D_MODEL].astype(F32))
    merged = merged + gb * _dot(o_dn, wpb_ref[...])
    gc = jax.nn.sigmoid(gates[:, 2 * D_MODEL:3 * D_MODEL].astype(F32))
    merged = merged + gc * _dot(oc_ref[0], wpc_ref[...])
    mix = _dot(merged.astype(BF16), wout_ref[...])
    g1 = mod_ref[0, 2:3, :]
    r = DN_ALPHA * x_ref[0] + g1 * mix
    o_ref[0] = _layer_norm(r, lng_ref[...], lnb_ref[...])


def _merge(o_a, o_f, o_b, y, o_c, x, mod, dn_norm, w_pa, w_pb, w_pc, w_out, ln_g, ln_b, layer):
    bm, lm, _ = x.shape
    tm = 512
    tok = lambda w, blk: pl.BlockSpec((1, tm, w), lambda b, i: (b, i, blk))
    wsp = lambda shp: pl.BlockSpec((None,) + shp, lambda b, i: (layer, 0, 0))
    return pl.pallas_call(
        _merge_kernel,
        grid=(bm, lm // tm),
        in_specs=[
            tok(512, 0), tok(512, 0), tok(512, 0), tok(512, DZ0 // 512), tok(512, 0),
            tok(3 * D_MODEL, 0), tok(D_MODEL, 0),
            pl.BlockSpec((1, 6, D_MODEL), lambda b, i: (b, 0, 0)),
            wsp((1, DN_DV)),
            wsp((A_WIDTH, D_MODEL)), wsp((DN_VW, D_MODEL)), wsp((512, D_MODEL)), wsp((D_MODEL, D_MODEL)),
            pl.BlockSpec((None, None, 1, D_MODEL), lambda b, i: (layer, 0, 0, 0)),
            pl.BlockSpec((None, None, 1, D_MODEL), lambda b, i: (layer, 0, 0, 0)),
        ],
        out_specs=tok(D_MODEL, 0),
        out_shape=jax.ShapeDtypeStruct((bm, lm, D_MODEL), F32),
        compiler_params=_cparams(("parallel", "parallel")),
        name="merge",
    )(o_a, o_f, o_b, y, o_c, y, x, mod, dn_norm, w_pa, w_pb, w_pc, w_out, ln_g, ln_b)


def _ffn_kernel(x_ref, mod_ref, wg_ref, wu_ref, wd_ref, lng_ref, lnb_ref, o_ref, h_ref, acc_ref, *, nf):
    f = pl.program_id(2)

    @pl.when(f == 0)
    def _():
        h_ref[...] = (x_ref[0] * (1.0 + mod_ref[0, 4:5, :]) + mod_ref[0, 3:4, :]).astype(BF16)
        acc_ref[...] = jnp.zeros(acc_ref.shape, F32)

    h = h_ref[...]
    act = (jax.nn.silu(_dot(h, wg_ref[...])) * _dot(h, wu_ref[...])).astype(BF16)
    acc_ref[...] += _dot(act, wd_ref[...])

    @pl.when(f == nf - 1)
    def _():
        r = DN_ALPHA * x_ref[0] + mod_ref[0, 5:6, :] * acc_ref[...]
        o_ref[0] = _layer_norm(r, lng_ref[...], lnb_ref[...])


def _ffn(x, mod, w_gu, w_down, ln_g, ln_b, layer, idx):
    bm, lm, _ = x.shape
    tm, tf = 1024, 256
    nf = D_FF // tf
    return pl.pallas_call(
        functools.partial(_ffn_kernel, nf=nf),
        grid=(bm, lm // tm, nf),
        in_specs=[
            pl.BlockSpec((1, tm, D_MODEL), lambda b, i, f: (b, i, 0)),
            pl.BlockSpec((1, 6, D_MODEL), lambda b, i, f: (b, 0, 0)),
            pl.BlockSpec((None, D_MODEL, tf), lambda b, i, f: (idx, 0, f)),
            pl.BlockSpec((None, D_MODEL, tf), lambda b, i, f: (idx, 0, nf + f)),
            pl.BlockSpec((None, tf, D_MODEL), lambda b, i, f: (idx, f, 0)),
            pl.BlockSpec((None, None, 1, D_MODEL), lambda b, i, f: (layer, 1, 0, 0)),
            pl.BlockSpec((None, None, 1, D_MODEL), lambda b, i, f: (layer, 1, 0, 0)),
        ],
        out_specs=pl.BlockSpec((1, tm, D_MODEL), lambda b, i, f: (b, i, 0)),
        out_shape=jax.ShapeDtypeStruct((bm, lm, D_MODEL), F32),
        scratch_shapes=[pltpu.VMEM((tm, D_MODEL), BF16), pltpu.VMEM((tm, D_MODEL), F32)],
        compiler_params=_cparams(("parallel", "parallel", "arbitrary")),
        name="ffn",
    )(x, mod, w_gu, w_gu, w_down, ln_g, ln_b)


def _route(h, wr, br):
    logits = _dot_f32(h, wr)
    lane = lax.broadcasted_iota(jnp.int32, logits.shape, 1)
    valid = lane < N_EXPERTS
    neg = -jnp.inf
    sel = jnp.where(valid, logits + br, neg)
    m1 = jnp.max(sel, -1, keepdims=True)
    i1 = jnp.min(jnp.where(sel == m1, lane, LANES), -1, keepdims=True)
    sel2 = jnp.where(lane == i1, neg, sel)
    m2 = jnp.max(sel2, -1, keepdims=True)
    i2 = jnp.min(jnp.where(sel2 == m2, lane, LANES), -1, keepdims=True)
    l1 = jnp.sum(jnp.where(lane == i1, logits, 0.0), -1, keepdims=True)
    l2 = jnp.sum(jnp.where(lane == i2, logits, 0.0), -1, keepdims=True)
    mx = jnp.maximum(l1, l2)
    e1 = jnp.exp(l1 - mx)
    e2 = jnp.exp(l2 - mx)
    den = e1 + e2
    comb = jnp.where(lane == i1, e1 / den, 0.0) + jnp.where(lane == i2, e2 / den, 0.0)
    return comb, (lane == i1) | (lane == i2)


MOE_TM = 1024
MOE_CAP0 = 320
MOE_CAP1 = 256


def _moe_kernel(x_ref, mod_ref, wr_ref, br_ref, tri_ref, wg_ref, wu_ref, wd_ref, lng_ref, lnb_ref, o_ref,
                h_ref, comb_ref, slot_ref, slot_t_ref, acc_ref):
    e = pl.program_id(2)
    tm = h_ref.shape[0]

    @pl.when(e == 0)
    def _():
        hf = x_ref[0] * (1.0 + mod_ref[0, 4:5, :]) + mod_ref[0, 3:4, :]
        h_ref[...] = hf.astype(BF16)
        comb, sel = _route(hf, wr_ref[...], br_ref[...])
        comb_ref[...] = comb
        sel16 = jnp.where(sel, 1.0, 0.0).astype(BF16)
        nb = tri_ref.shape[0]
        carry = jnp.zeros((1, LANES), F32)
        for blk in range(tm // nb):
            rows = slice(blk * nb, (blk + 1) * nb)
            rank = _dot(tri_ref[...], sel16[rows]) + carry
            slot_ref[rows, :] = jnp.where(sel[rows], rank, -1.0)
            carry = carry + jnp.sum(sel16[rows].astype(F32), 0, keepdims=True)
        slot_t_ref[...] = slot_ref[...].T
        acc_ref[...] = jnp.zeros(acc_ref.shape, F32)

    lane = lax.broadcasted_iota(jnp.int32, comb_ref.shape, 1)
    pick = lane == e
    c_col = jnp.sum(jnp.where(pick, comb_ref[...], 0.0), -1, keepdims=True)
    slot_col = jnp.max(jnp.where(pick, slot_ref[...], -1.0), -1, keepdims=True)
    slot_row = slot_t_ref[pl.ds(e, 1), :]
    load = jnp.max(slot_row) + 1.0

    def block(base, cap):
        r_i = lax.broadcasted_iota(jnp.int32, (cap, tm), 0).astype(F32) + base
        gather = jnp.where(r_i == slot_row, 1.0, 0.0).astype(BF16)
        xe = _dot(gather, h_ref[...]).astype(BF16)
        act = (jax.nn.silu(_dot(xe, wg_ref[...])) * _dot(xe, wu_ref[...])).astype(BF16)
        ye = _dot(act, wd_ref[...]).astype(BF16)
        c_i = lax.broadcasted_iota(jnp.int32, (tm, cap), 1).astype(F32) + base
        scatter = jnp.where(c_i == slot_col, 1.0, 0.0).astype(BF16)
        acc_ref[...] += c_col * _dot(scatter, ye)

    block(0.0, MOE_CAP0)
    base = MOE_CAP0
    while base < tm:
        pl.when(load > base)(functools.partial(block, float(base), MOE_CAP1))
        base += MOE_CAP1

    @pl.when(e == N_EXPERTS - 1)
    def _():
        r = DN_ALPHA * x_ref[0] + mod_ref[0, 5:6, :] * acc_ref[...]
        o_ref[0] = _layer_norm(r, lng_ref[...], lnb_ref[...])


def _moe(x, mod, w_router_p, b_router_p, tri, w_gu, w_down, ln_g, ln_b, layer, idx):
    bm, lm, _ = x.shape
    tm = MOE_TM
    nb = tri.shape[0]
    return pl.pallas_call(
        _moe_kernel,
        grid=(bm, lm // tm, N_EXPERTS),
        in_specs=[
            pl.BlockSpec((1, tm, D_MODEL), lambda b, i, e: (b, i, 0)),
            pl.BlockSpec((1, 6, D_MODEL), lambda b, i, e: (b, 0, 0)),
            pl.BlockSpec((None, D_MODEL, LANES), lambda b, i, e: (idx, 0, 0)),
            pl.BlockSpec((None, 1, LANES), lambda b, i, e: (idx, 0, 0)),
            pl.BlockSpec((nb, nb), lambda b, i, e: (0, 0)),
            pl.BlockSpec((None, None, D_MODEL, D_FF_EXPERT), lambda b, i, e: (idx, e, 0, 0)),
            pl.BlockSpec((None, None, D_MODEL, D_FF_EXPERT), lambda b, i, e: (idx, e, 0, 1)),
            pl.BlockSpec((None, None, D_FF_EXPERT, D_MODEL), lambda b, i, e: (idx, e, 0, 0)),
            pl.BlockSpec((None, None, 1, D_MODEL), lambda b, i, e: (layer, 1, 0, 0)),
            pl.BlockSpec((None, None, 1, D_MODEL), lambda b, i, e: (layer, 1, 0, 0)),
        ],
        out_specs=pl.BlockSpec((1, tm, D_MODEL), lambda b, i, e: (b, i, 0)),
        out_shape=jax.ShapeDtypeStruct((bm, lm, D_MODEL), F32),
        scratch_shapes=[pltpu.VMEM((tm, D_MODEL), BF16), pltpu.VMEM((tm, LANES), F32),
                        pltpu.VMEM((tm, LANES), F32), pltpu.VMEM((LANES, tm), F32),
                        pltpu.VMEM((tm, D_MODEL), F32)],
        compiler_params=_cparams(("parallel", "parallel", "arbitrary")),
        name="moe",
    )(x, mod, w_router_p, b_router_p, tri, w_gu, w_gu, w_down, ln_g, ln_b)


def _pack_w_in(w_in):
    d = w_in.shape[0]
    o = 0
    cols = {}
    for name, width in (("a_u", A_WIDTH), ("a_v", A_WIDTH), ("dq", DN_QK), ("dk", DN_QK), ("dv", DN_VW),
                        ("dz", DN_VW), ("sm", 4 * DN_HEADS), ("cq", Q_LORA), ("ckv", KV_LORA), ("kr", MLA_ROPE),
                        ("ga", D_MODEL), ("gb", D_MODEL), ("gc", D_MODEL)):
        cols[name] = w_in[:, :, o:o + width]
        o += width
    zpad = lambda n: jnp.zeros((d, D_MODEL, n), w_in.dtype)
    packed = jnp.concatenate(
        [cols["ga"], cols["gb"], cols["gc"], cols["dq"], cols["dk"], cols["dv"], cols["dz"],
         cols["a_u"], cols["a_v"], cols["cq"], cols["ckv"],
         cols["kr"], zpad(LANES - MLA_ROPE), cols["sm"], zpad(LANES - 4 * DN_HEADS)], axis=-1)
    return packed.astype(BF16)


def _rope_tables(n_tok):
    row = (jnp.arange(n_tok) // GRID_W).astype(F32)
    col = (jnp.arange(n_tok) % GRID_W).astype(F32)
    inv_freq = ROPE_BASE ** (-jnp.arange(AXIS_FREQS, dtype=F32) / AXIS_FREQS)
    ang_r = row[:, None] * inv_freq
    ang_c = col[:, None] * inv_freq
    ones = jnp.ones((n_tok, LANES - MLA_ROPE), F32)
    cos = jnp.concatenate([jnp.cos(ang_r), jnp.cos(ang_r), jnp.cos(ang_c), jnp.cos(ang_c), ones], axis=1)
    sin = jnp.concatenate([-jnp.sin(ang_r), jnp.sin(ang_r), -jnp.sin(ang_c), jnp.sin(ang_c), 0.0 * ones], axis=1)
    return cos, sin


def kernel(x_prompt, x_sample, state_dn, cache_ckv, cache_krope, c, c_ctx, w_mod, b_mod, w_in, a_ln_g, a_ln_b, a_ws, a_bs, dn_conv, dn_a_log, dn_dt_bias, dn_norm, q_norm, w_qb, kv_norm, w_kvb, w_pa, w_pb, w_pc, w_out, ln_g, ln_b, ffn_gu, ffn_down, moe_router, moe_bias, moe_gu, moe_down):
    batch, seq, _ = x_prompt.shape
    dec_batch, dec_seq, _ = x_sample.shape
    depth = w_in.shape[0]
    assert depth == DEPTH

    w_in_p = _pack_w_in(w_in)
    a_bs_t = jnp.swapaxes(a_bs, 1, 2)
    ln_a_g = a_ln_g.reshape(depth, 1, A_WIDTH)
    ln_a_b = a_ln_b.reshape(depth, 1, A_WIDTH)
    lane_pad = lambda v: jnp.pad(v.reshape(depth, 1, -1), ((0, 0), (0, 0), (0, LANES - v[0].size)))
    alog_row = lane_pad(dn_a_log)
    dtb_row = lane_pad(dn_dt_bias)
    dn_g = dn_norm.reshape(depth, 1, DN_DV)
    qg = q_norm.reshape(depth, 1, Q_LORA)
    kvg = kv_norm.reshape(depth, 1, KV_LORA)
    wq = w_qb.reshape(depth, Q_LORA, MLA_HEADS, MLA_NOPE + MLA_ROPE)
    wqn = wq[..., :MLA_NOPE].reshape(depth, Q_LORA, MLA_HEADS * MLA_NOPE).astype(BF16)
    wqr = jnp.pad(wq[..., MLA_NOPE:], ((0, 0), (0, 0), (0, 0), (0, LANES - MLA_ROPE)))
    wqr = wqr.reshape(depth, Q_LORA, MLA_HEADS * LANES).astype(BF16)
    wkv = w_kvb.reshape(depth, KV_LORA, MLA_HEADS, MLA_NOPE + MLA_V)
    wkn = wkv[..., :MLA_NOPE].reshape(depth, KV_LORA, MLA_HEADS * MLA_NOPE).astype(BF16)
    wv = wkv[..., MLA_NOPE:].reshape(depth, KV_LORA, MLA_HEADS * MLA_V).astype(BF16)
    w_pa16, w_pb16, w_pc16, w_out16 = (w.astype(BF16) for w in (w_pa, w_pb, w_pc, w_out))
    ffn_gu16, ffn_down16 = ffn_gu.astype(BF16), ffn_down.astype(BF16)
    moe_gu16, moe_down16 = moe_gu.astype(BF16), moe_down.astype(BF16)
    n_moe = moe_router.shape[0]
    wr_p = jnp.pad(moe_router, ((0, 0), (0, 0), (0, LANES - N_EXPERTS)))
    br_p = jnp.pad(moe_bias.reshape(n_moe, 1, N_EXPERTS), ((0, 0), (0, 0), (0, LANES - N_EXPERTS)))
    ln_g4 = ln_g.reshape(depth, 2, 1, D_MODEL)
    ln_b4 = ln_b.reshape(depth, 2, 1, D_MODEL)
    rope_tabs = _rope_tables(dec_seq)
    gdn_tabs = _gdn_tables(GDN_TILE)
    moe_tri = (jnp.arange(256)[None, :] < jnp.arange(256)[:, None]).astype(BF16)

    n_cond = 16
    cond = jnp.concatenate([c, c_ctx[None, :], jnp.zeros((n_cond - dec_batch - 1, D_MODEL), F32)], axis=0)
    mods = _adaln(cond, w_mod, b_mod).reshape(depth, n_cond, 6, D_MODEL)

    xp = x_prompt.reshape(1, batch * seq, D_MODEL)
    xs = x_sample
    st_dn, st_ckv, st_kr = [], [], []

    def channel_mixer(x, mod, l):
        if l % 2 == 0:
            return _ffn(x, mod, ffn_gu16, ffn_down16, ln_g4, ln_b4, l, l // 2)
        return _moe(x, mod, wr_p, br_p, moe_tri, moe_gu16, moe_down16, ln_g4, ln_b4, l, l // 2)

    def token_mixer(x, mod, l, bsz, n_tok, latent):
        y = _inproj(x, mod, w_in_p, l)
        o_a = _gmlp(y.reshape(-1, NP), a_ws, a_bs_t, ln_a_g, ln_a_b, l).reshape(x.shape[0], x.shape[1], A_WIDTH)
        ys = y.reshape(bsz, n_tok, NP)
        o_f, o_b, s_fin = _gdn(ys, dn_conv, alog_row, dtb_row, gdn_tabs, state_dn if latent else None, l)
        if latent:
            q, k, v = _mla_prep(ys, qg, kvg, wqn, wqr, wkn, wv, l, rope_tabs, False)
            ctx_kv = _mla_ctx(cache_ckv, cache_krope, wkn, wv, l)
            o_c = _attention(q, k, v, ctx_kv, 512, 512)
            extra = None
        else:
            q, k, v, ckv_n, kr = _mla_prep(ys, qg, kvg, wqn, wqr, wkn, wv, l, None, True)
            o_c = _attention(q, k, v, None, n_tok, n_tok)
            extra = (s_fin, ckv_n, kr)
        shp = (x.shape[0], x.shape[1], 512)
        x = _merge(o_a, o_f.reshape(shp), o_b.reshape(shp), y, o_c.reshape(shp), x, mod, dn_g,
                   w_pa16, w_pb16, w_pc16, w_out16, ln_g4, ln_b4, l)
        return x, extra

    for l in range(depth):
        mod_p = mods[l, dec_batch:dec_batch + 1]
        mod_s = mods[l, :dec_batch]
        xp, (s_fin, ckv_n, kr) = token_mixer(xp, mod_p, l, batch, seq, False)
        xp = channel_mixer(xp, mod_p, l)
        st_dn.append(s_fin)
        st_ckv.append(ckv_n)
        st_kr.append(kr)
        xs, _ = token_mixer(xs, mod_s, l, dec_batch, dec_seq, True)
        xs = channel_mixer(xs, mod_s, l)

    return (xp.reshape(batch, seq, D_MODEL), xs,
            jnp.stack(st_dn, axis=1), jnp.stack(st_ckv, axis=1), jnp.stack(st_kr, axis=1))
```

```python
import functools
import math

import jax
import jax.numpy as jnp
from jax import lax
from jax.experimental import pallas as pl
from jax.experimental.pallas import tpu as pltpu

F32 = jnp.float32
BF16 = jnp.bfloat16

D_MODEL = 1024
DEPTH = 4
GRID_W = 64
A_WIDTH = 512
A_CHUNK = 128
A_GROUPS = 4
DN_HEADS = 4
DN_DK = 128
DN_DV = 128
DN_QK = DN_HEADS * DN_DK
DN_VW = DN_HEADS * DN_DV
DN_CONV_K = 5
GDN_TILE = 256
MLA_HEADS = 4
MLA_NOPE = 128
MLA_ROPE = 64
MLA_V = 128
Q_LORA = 256
KV_LORA = 256
AXIS_FREQS = MLA_ROPE // 4
ROPE_BASE = 10000.0
D_FF = 2816
N_EXPERTS = 8
D_FF_EXPERT = D_FF // 2
DN_ALPHA = (2 * DEPTH) ** 0.25
EPS = 1e-6
LN_EPS = 1e-5

LANES = 128
HALO = 16
VMEM_LIMIT = 56 * 1024 * 1024

GATE0 = 0
QKV0 = 3 * D_MODEL
DZ0 = QKV0 + 3 * DN_QK
AU0 = DZ0 + DN_VW
AV0 = AU0 + A_WIDTH
CQ0 = AV0 + A_WIDTH
CKV0 = CQ0 + Q_LORA
KR0 = CKV0 + KV_LORA
SM0 = KR0 + LANES
NP = SM0 + LANES
QKV_W = 3 * DN_QK


def _cparams(sem):
    return pltpu.CompilerParams(dimension_semantics=sem, vmem_limit_bytes=VMEM_LIMIT)


def _dot(a, b):
    return jnp.dot(a, b, preferred_element_type=F32)


def _dot_nt(a, b):
    return lax.dot_general(a, b, (((1,), (1,)), ((), ())), preferred_element_type=F32)


def _dot_tn(a, b):
    return lax.dot_general(a, b, (((0,), (0,)), ((), ())), preferred_element_type=F32)


def _split3(x):
    hi = x.astype(BF16)
    r = x - hi.astype(F32)
    mid = r.astype(BF16)
    lo = (r - mid.astype(F32)).astype(BF16)
    return hi, mid, lo


def _dot_f32(a, b):
    ah, am, al = _split3(a)
    bh, bm, bl = _split3(b)
    return (_dot(ah, bh) + (_dot(ah, bm) + _dot(am, bh))
            + (_dot(am, bm) + _dot(ah, bl) + _dot(al, bh)))


def _dot_hi(a, b):
    ah, am, _ = _split3(a)
    bh, bm, _ = _split3(b)
    return _dot(ah, bh) + (_dot(ah, bm) + _dot(am, bh))


def _layer_norm(r, g, b):
    mu = jnp.mean(r, -1, keepdims=True)
    rc = r - mu
    var = jnp.mean(rc * rc, -1, keepdims=True)
    return rc * lax.rsqrt(var + LN_EPS) * g + b


def _adaln_kernel(c_ref, w_ref, b_ref, o_ref):
    s = jax.nn.silu(c_ref[...])
    o_ref[...] = _dot_f32(s, w_ref[...]) + b_ref[...]


def _adaln(cond, w_mod, b_mod):
    rows = cond.shape[0]
    n_out = w_mod.shape[-1]
    tn = 1536
    return pl.pallas_call(
        _adaln_kernel,
        grid=(DEPTH, n_out // tn),
        in_specs=[
            pl.BlockSpec((rows, D_MODEL), lambda l, j: (0, 0)),
            pl.BlockSpec((None, D_MODEL, tn), lambda l, j: (l, 0, j)),
            pl.BlockSpec((None, 1, tn), lambda l, j: (l, 0, j)),
        ],
        out_specs=pl.BlockSpec((None, rows, tn), lambda l, j: (l, 0, j)),
        out_shape=jax.ShapeDtypeStruct((DEPTH, rows, n_out), F32),
        compiler_params=_cparams(("parallel", "parallel")),
        name="adaln",
    )(cond, w_mod, b_mod.reshape(DEPTH, 1, n_out))


def _inproj_kernel(x_ref, mod_ref, w_ref, y_ref, h_ref):
    @pl.when(pl.program_id(2) == 0)
    def _():
        sh = mod_ref[0, 0:1, :]
        sc = mod_ref[0, 1:2, :]
        h_ref[...] = (x_ref[0] * (1.0 + sc) + sh).astype(BF16)

    y_ref[0] = _dot(h_ref[...], w_ref[...]).astype(y_ref.dtype)


def _inproj(x, mod, w_in_p, layer):
    bm, lm, _ = x.shape
    tm, tn = 1024, 1152
    return pl.pallas_call(
        _inproj_kernel,
        grid=(bm, lm // tm, NP // tn),
        in_specs=[
            pl.BlockSpec((1, tm, D_MODEL), lambda b, i, j: (b, i, 0)),
            pl.BlockSpec((1, 6, D_MODEL), lambda b, i, j: (b, 0, 0)),
            pl.BlockSpec((None, D_MODEL, tn), lambda b, i, j: (layer, 0, j)),
        ],
        out_specs=pl.BlockSpec((1, tm, tn), lambda b, i, j: (b, i, j)),
        out_shape=jax.ShapeDtypeStruct((bm, lm, NP), BF16),
        scratch_shapes=[pltpu.VMEM((tm, D_MODEL), BF16)],
        compiler_params=_cparams(("parallel", "parallel", "arbitrary")),
        name="inproj",
    )(x, mod, w_in_p)


def _gmlp_kernel(u_ref, v_ref, ws_ref, bs_ref, g_ref, b_ref, o_ref):
    tl = u_ref.shape[0]
    gd = A_WIDTH // A_GROUPS
    v = jax.nn.gelu(v_ref[...].astype(F32))
    vn = _layer_norm(v, g_ref[...], b_ref[...]).astype(BF16)
    u = jax.nn.gelu(u_ref[...].astype(F32))
    for n in range(tl // A_CHUNK):
        rows = slice(n * A_CHUNK, (n + 1) * A_CHUNK)
        for g in range(A_GROUPS):
            cols = slice(g * gd, (g + 1) * gd)
            mixed = _dot(ws_ref[g].astype(BF16), vn[rows, cols]) + bs_ref[:, g:g + 1]
            o_ref[rows, cols] = (u[rows, cols] * mixed).astype(o_ref.dtype)


def _gmlp(y2d, a_ws, a_bs_t, ln_g, ln_b, layer):
    t = y2d.shape[0]
    tl = 512
    return pl.pallas_call(
        _gmlp_kernel,
        grid=(t // tl,),
        in_specs=[
            pl.BlockSpec((tl, A_WIDTH), lambda i: (i, AU0 // A_WIDTH)),
            pl.BlockSpec((tl, A_WIDTH), lambda i: (i, AV0 // A_WIDTH)),
            pl.BlockSpec((None, A_GROUPS, A_CHUNK, A_CHUNK), lambda i: (layer, 0, 0, 0)),
            pl.BlockSpec((None, A_CHUNK, A_GROUPS), lambda i: (layer, 0, 0)),
            pl.BlockSpec((None, 1, A_WIDTH), lambda i: (layer, 0, 0)),
            pl.BlockSpec((None, 1, A_WIDTH), lambda i: (layer, 0, 0)),
        ],
        out_specs=pl.BlockSpec((tl, A_WIDTH), lambda i: (i, 0)),
        out_shape=jax.ShapeDtypeStruct((t, A_WIDTH), BF16),
        compiler_params=_cparams(("parallel",)),
        name="gmlp",
    )(y2d, y2d, a_ws, a_bs_t, ln_g, ln_b)


GDN_SCRATCH = {"kb": DN_DK, "k": DN_DK, "q": DN_DK, "qe": DN_DK, "kdec": DN_DK, "rhs": DN_DV + DN_DK,
               "low": None, "qk": None, "inv": None}


def _gdn_stage1(d, reverse, is_first, is_last, qkv_ref, prev_ref, next_ref, sm_ref, conv_ref, alog_ref,
                dtb_ref, tri_ref, lvl_ref, ext_ref, act_ref, tile, conv_now, o_ref, tl, sc, probs):
    c = tl
    off = HALO - DN_CONV_K // 2
    st = {}

    def fill():
        ext_ref[0:HALO, :] = jnp.where(is_first, 0.0, prev_ref[0].astype(F32))
        ext_ref[HALO:HALO + tl, :] = qkv_ref[0].astype(F32)
        ext_ref[HALO + tl:2 * HALO + tl, :] = jnp.where(is_last, 0.0, next_ref[0].astype(F32))

    rb = 32

    def conv(part, r0):
        cols = slice(part * DN_QK, (part + 1) * DN_QK)
        acc = conv_ref[0:1, cols] * ext_ref[off + r0:off + r0 + rb, cols]
        for j in range(1, DN_CONV_K):
            acc = acc + conv_ref[j:j + 1, cols] * ext_ref[off + j + r0:off + j + r0 + rb, cols]
        y = jax.nn.silu(acc)
        if part < 2:
            for h in range(DN_HEADS):
                yh = y[:, h * DN_DK:(h + 1) * DN_DK]
                yh = yh * lax.rsqrt(jnp.sum(yh * yh, -1, keepdims=True) + EPS)
                act_ref[tile, r0:r0 + rb, part * DN_QK + h * DN_DK:part * DN_QK + (h + 1) * DN_DK] = (
                    yh.astype(act_ref.dtype))
        else:
            act_ref[tile, r0:r0 + rb, cols] = y.astype(act_ref.dtype)

    def conv_all():
        fill()
        for part in range(3):
            for r0 in range(0, tl, rb):
                conv(part, r0)

    def maybe_conv():
        if conv_now is not None:
            pl.when(conv_now)(conv_all)

    def gates():
        sm = sm_ref[0].astype(F32)
        g_all = -jnp.exp(alog_ref[...]) * jax.nn.softplus(sm + dtb_ref[...])
        st["beta"] = jax.nn.sigmoid(sm)
        tri = tri_ref[d]
        gh, gm, gl = _split3(g_all)
        st["gam"] = _dot(tri, gh) + _dot(tri, gm) + _dot(tri, gl)
        st["gam_t"] = st["gam"].T
        ii = lax.broadcasted_iota(jnp.int32, (c, c), 0)
        jj = lax.broadcasted_iota(jnp.int32, (c, c), 1)
        st["incl"] = jnp.where((jj >= ii) if reverse else (jj <= ii), 1.0, 0.0).astype(F32)
        st["strict"] = jnp.where((jj > ii) if reverse else (jj < ii), 1.0, 0.0).astype(F32)
        st["eye"] = jnp.where(ii == jj, 1.0, 0.0).astype(F32)

    def head(h):
        scale = DN_DK ** -0.5
        lane = d * DN_HEADS + h
        pi = lane
        gam, gam_t, beta = st["gam"], st["gam_t"], st["beta"]
        gr = gam_t[lane:lane + 1, :]
        g_last = gam[0:1, lane:lane + 1] if reverse else gam[c - 1:c, lane:lane + 1]
        ob = 64
        for r0 in range(0, c, ob):
            rows = slice(r0, r0 + ob)
            gc = gam[rows, lane:lane + 1]
            bcol = beta[rows, 2 * DN_HEADS + lane:2 * DN_HEADS + lane + 1]
            eg = jnp.exp(gc)
            qh = act_ref[tile, rows, h * DN_DK:(h + 1) * DN_DK].astype(F32) * scale
            kh = act_ref[tile, rows, DN_QK + h * DN_DK:DN_QK + (h + 1) * DN_DK].astype(F32)
            vh = act_ref[tile, rows, 2 * DN_QK + h * DN_DV:2 * DN_QK + (h + 1) * DN_DV].astype(F32)
            kb = kh * bcol
            sc["kb"][pi, rows, :] = kb.astype(BF16)
            sc["k"][pi, rows, :] = kh.astype(BF16)
            sc["q"][pi, rows, :] = qh.astype(BF16)
            sc["qe"][pi, rows, :] = (qh * eg).astype(BF16)
            sc["kdec"][pi, rows, :] = (kh * jnp.exp(g_last - gc)).astype(BF16)
            sc["rhs"][pi, rows, 0:DN_DV] = (vh * bcol).astype(BF16)
            sc["rhs"][pi, rows, DN_DV:2 * DN_DV] = (kb * eg).astype(BF16)
        kk = _dot_nt(sc["kb"][pi], sc["k"][pi])
        qk = _dot_nt(sc["q"][pi], sc["k"][pi])
        mb = 32
        for r0 in range(0, c, mb):
            rows = slice(r0, r0 + mb)
            gc = gam[rows, lane:lane + 1]
            dec = jnp.exp(jnp.minimum(gc - gr, 0.0))
            low = kk[rows] * dec * st["strict"][rows]
            sc["qk"][pi, rows, :] = (qk[rows] * dec * st["incl"][rows]).astype(BF16)
            sc["low"][pi, rows, :] = low.astype(BF16)
            sc["inv"][pi, rows, :] = (st["eye"][rows] - low * lvl_ref[d, 0, rows, :]).astype(BF16)
        probs.append(dict(d=d, h=h, pi=pi, o_ref=o_ref, e_last=jnp.exp(g_last)))

    return [maybe_conv, gates] + [functools.partial(head, h) for h in range(DN_HEADS)]


def _gdn_level(probs, lvl_ref, sc, lv):
    for p in probs:
        p["t16"] = _dot(sc["low"][p["pi"]], sc["inv"][p["pi"]]).astype(BF16)
    for p in probs:
        x16 = _dot(sc["inv"][p["pi"]], p["t16"]).astype(BF16)
        sc["inv"][p["pi"]] = sc["inv"][p["pi"]] - x16 * lvl_ref[p["d"], lv]


def _gdn_solve(p, sc):
    sol = _dot(sc["inv"][p["pi"]], sc["rhs"][p["pi"]])
    p["u"] = sol[:, :DN_DV]
    p["w16"] = sol[:, DN_DV:].astype(BF16)


def _gdn_scan(p, s_ref, sc):
    st = s_ref[p["d"], p["h"]]
    st16 = st.astype(BF16)
    v16 = (p["u"] - _dot(p["w16"], st16)).astype(BF16)
    o = _dot(sc["qe"][p["pi"]], st16) + _dot(sc["qk"][p["pi"]], v16)
    s_ref[p["d"], p["h"]] = st * p["e_last"] + _dot_tn(sc["kdec"][p["pi"]], v16)
    p["o_ref"][0, :, p["h"] * DN_DV:(p["h"] + 1) * DN_DV] = o.astype(p["o_ref"].dtype)


def _gdn_kernel(*refs, tl, n_t, has_init):
    (qkv_f, prev_f, next_f, sm_f, qkv_b, prev_b, next_b, sm_b,
     conv_ref, alog_ref, dtb_ref, tri_ref, lvl_ref) = refs[:13]
    k = 13
    if has_init:
        s0_ref = refs[k]
        k += 1
    o_f, o_b, sfin_ref, ext_f, ext_b, act_c, s_ref = refs[k:k + 7]
    sc = dict(zip(GDN_SCRATCH, refs[k + 7:]))
    i = pl.program_id(1)
    n_lvl = lvl_ref.shape[1]
    first_visit = i < (n_t + 1) // 2
    conv_b = None if n_t == 1 else first_visit

    @pl.when(i == 0)
    def _():
        if has_init:
            s_ref[...] = s0_ref[0]
        else:
            s_ref[...] = jnp.zeros(s_ref.shape, F32)

    probs_f, probs_b = [], []
    items_f = _gdn_stage1(0, False, i == 0, i == n_t - 1, qkv_f, prev_f, next_f, sm_f, conv_ref, alog_ref,
                          dtb_ref, tri_ref, lvl_ref, ext_f, act_c, i, first_visit, o_f, tl, sc, probs_f)
    items_b = _gdn_stage1(1, True, i == n_t - 1, i == 0, qkv_b, prev_b, next_b, sm_b, conv_ref, alog_ref,
                          dtb_ref, tri_ref, lvl_ref, ext_b, act_c, n_t - 1 - i, conv_b, o_b, tl, sc, probs_b)
    for item in items_f + items_b:
        item()
    probs = probs_f + probs_b
    for lv in range(1, n_lvl):
        _gdn_level(probs, lvl_ref, sc, lv)
    for p in probs:
        _gdn_solve(p, sc)
    for p in probs:
        _gdn_scan(p, s_ref, sc)

    @pl.when(i == n_t - 1)
    def _():
        sfin_ref[0] = s_ref[...]


def _gdn_tables(c):
    ii = jnp.arange(c)[:, None]
    jj = jnp.arange(c)[None, :]
    tri = jnp.stack([jj <= ii, jj >= ii]).astype(BF16)
    x = ii ^ jj
    lvls = []
    s = 1
    while s < c:
        pair = (x >= s) & (x < 2 * s)
        lvls.append(jnp.stack([pair & (jj < ii), pair & (jj > ii)]))
        s *= 2
    return tri, jnp.stack(lvls, axis=1).astype(BF16)


def _gdn(y, conv_w, alog_row, dtb_row, tables, s0, layer):
    b, l, _ = y.shape
    tl = GDN_TILE
    n_t = l // tl
    assert n_t == 1 or n_t % 2 == 0
    hb = tl // HALO
    n_hb = l // HALO
    qkv_blk = QKV0 // QKV_W
    sm_blk = SM0 // LANES
    has_init = s0 is not None

    def fwd(i):
        return i

    def bwd(i):
        return n_t - 1 - i

    def tile_specs(pos):
        return [
            pl.BlockSpec((1, tl, QKV_W), lambda bb, i: (bb, pos(i), qkv_blk)),
            pl.BlockSpec((1, HALO, QKV_W), lambda bb, i: (bb, jnp.maximum(pos(i) * hb - 1, 0), qkv_blk)),
            pl.BlockSpec((1, HALO, QKV_W), lambda bb, i: (bb, jnp.minimum((pos(i) + 1) * hb, n_hb - 1), qkv_blk)),
            pl.BlockSpec((1, tl, LANES), lambda bb, i: (bb, pos(i), sm_blk)),
        ]

    tri, lvls = tables
    n_lvl = lvls.shape[1]
    in_specs = tile_specs(fwd) + tile_specs(bwd) + [
        pl.BlockSpec((None, DN_CONV_K, QKV_W), lambda bb, i: (layer, 0, 0)),
        pl.BlockSpec((None, 1, LANES), lambda bb, i: (layer, 0, 0)),
        pl.BlockSpec((None, 1, LANES), lambda bb, i: (layer, 0, 0)),
        pl.BlockSpec((2, tl, tl), lambda bb, i: (0, 0, 0)),
        pl.BlockSpec((2, n_lvl, tl, tl), lambda bb, i: (0, 0, 0, 0)),
    ]
    args = [y] * 8 + [conv_w, alog_row, dtb_row, tri, lvls]
    if has_init:
        in_specs.append(pl.BlockSpec((1, None, 2, DN_HEADS, DN_DK, DN_DV), lambda bb, i: (bb, layer, 0, 0, 0, 0)))
        args.append(s0)
    return pl.pallas_call(
        functools.partial(_gdn_kernel, tl=tl, n_t=n_t, has_init=has_init),
        grid=(b, n_t),
        in_specs=in_specs,
        out_specs=[
            pl.BlockSpec((1, tl, DN_VW), lambda bb, i: (bb, i, 0)),
            pl.BlockSpec((1, tl, DN_VW), lambda bb, i: (bb, n_t - 1 - i, 0)),
            pl.BlockSpec((1, 2, DN_HEADS, DN_DK, DN_DV), lambda bb, i: (bb, 0, 0, 0, 0)),
        ],
        out_shape=[
            jax.ShapeDtypeStruct((b, l, DN_VW), BF16),
            jax.ShapeDtypeStruct((b, l, DN_VW), BF16),
            jax.ShapeDtypeStruct((b, 2, DN_HEADS, DN_DK, DN_DV), F32),
        ],
        scratch_shapes=[
            pltpu.VMEM((tl + 2 * HALO, QKV_W), F32),
            pltpu.VMEM((tl + 2 * HALO, QKV_W), F32),
            pltpu.VMEM((n_t, tl, QKV_W), BF16),
            pltpu.VMEM((2, DN_HEADS, DN_DK, DN_DV), F32),
        ] + [pltpu.VMEM((2 * DN_HEADS, tl, w or tl), BF16) for w in GDN_SCRATCH.values()],
        compiler_params=_cparams(("parallel", "arbitrary")),
        name="gdn",
    )(*args)


def _rms(x, g):
    return x * lax.rsqrt(jnp.mean(x * x, -1, keepdims=True) + EPS) * g


def _rope128(x, cos, sin):
    lane = lax.broadcasted_iota(jnp.int32, x.shape, 1)
    first = (lane % (2 * AXIS_FREQS)) < AXIS_FREQS
    sw = jnp.where(first, pltpu.roll(x, LANES - AXIS_FREQS, 1), pltpu.roll(x, AXIS_FREQS, 1))
    return x * cos + sw * sin


def _mla_prep_kernel(*refs, rope, emit_cache):
    cq_ref, ckv_ref, kr_ref, qg_ref, kvg_ref, wqn_ref, wqr_ref, wkn_ref, wv_ref = refs[:9]
    k = 9
    if rope:
        cos_ref, sin_ref = refs[k:k + 2]
        k += 2
    q_out, k_out, v_out = refs[k:k + 3]
    k += 3
    scale = (MLA_NOPE + MLA_ROPE) ** -0.5
    cqn = _rms(cq_ref[0].astype(F32), qg_ref[...]).astype(BF16)
    qn = _dot(cqn, wqn_ref[...]) * scale
    qr = _dot(cqn, wqr_ref[...]) * scale
    ckvn = _rms(ckv_ref[0].astype(F32), kvg_ref[...])
    ckvn16 = ckvn.astype(BF16)
    kn = _dot(ckvn16, wkn_ref[...])
    v_out[0] = _dot(ckvn16, wv_ref[...]).T.astype(v_out.dtype)
    kr = kr_ref[0].astype(F32)
    if emit_cache:
        ckvn_out, kr_out = refs[k:k + 2]
        ckvn_out[0] = ckvn
        kr_out[0] = kr[:, :MLA_ROPE]
    if rope:
        cos = cos_ref[...]
        sin = sin_ref[...]
        kr = _rope128(kr, cos, sin)
    kr16 = kr.astype(k_out.dtype)
    for h in range(MLA_HEADS):
        qrh = qr[:, h * LANES:(h + 1) * LANES]
        if rope:
            qrh = _rope128(qrh, cos, sin)
        q_out[0, :, h * 256:h * 256 + 128] = qn[:, h * 128:(h + 1) * 128].astype(q_out.dtype)
        q_out[0, :, h * 256 + 128:(h + 1) * 256] = qrh.astype(q_out.dtype)
        k_out[0, :, h * 256:h * 256 + 128] = kn[:, h * 128:(h + 1) * 128].astype(k_out.dtype)
        k_out[0, :, h * 256 + 128:(h + 1) * 256] = kr16


def _mla_prep(y, q_norm, kv_norm, wqn, wqr, wkn, wv, layer, rope_tabs, emit_cache):
    b, l, _ = y.shape
    tm = 256
    rope = rope_tabs is not None
    w_spec = lambda shp: pl.BlockSpec((None,) + shp, lambda bb, i: (layer, 0, 0))
    in_specs = [
        pl.BlockSpec((1, tm, Q_LORA), lambda bb, i: (bb, i, CQ0 // Q_LORA)),
        pl.BlockSpec((1, tm, KV_LORA), lambda bb, i: (bb, i, CKV0 // KV_LORA)),
        pl.BlockSpec((1, tm, LANES), lambda bb, i: (bb, i, KR0 // LANES)),
        w_spec((1, Q_LORA)), w_spec((1, KV_LORA)),
        w_spec((Q_LORA, 512)), w_spec((Q_LORA, 512)), w_spec((KV_LORA, 512)), w_spec((KV_LORA, 512)),
    ]
    args = [y, y, y, q_norm, kv_norm, wqn, wqr, wkn, wv]
    if rope:
        in_specs += [pl.BlockSpec((tm, LANES), lambda bb, i: (i, 0))] * 2
        args += list(rope_tabs)
    out_specs = [
        pl.BlockSpec((1, tm, 1024), lambda bb, i: (bb, i, 0)),
        pl.BlockSpec((1, tm, 1024), lambda bb, i: (bb, i, 0)),
        pl.BlockSpec((1, 512, tm), lambda bb, i: (bb, 0, i)),
    ]
    out_shape = [
        jax.ShapeDtypeStruct((b, l, 1024), BF16),
        jax.ShapeDtypeStruct((b, l, 1024), BF16),
        jax.ShapeDtypeStruct((b, 512, l), BF16),
    ]
    if emit_cache:
        out_specs += [pl.BlockSpec((1, tm, KV_LORA), lambda bb, i: (bb, i, 0)),
                      pl.BlockSpec((1, tm, MLA_ROPE), lambda bb, i: (bb, i, 0))]
        out_shape += [jax.ShapeDtypeStruct((b, l, KV_LORA), F32),
                      jax.ShapeDtypeStruct((b, l, MLA_ROPE), F32)]
    return pl.pallas_call(
        functools.partial(_mla_prep_kernel, rope=rope, emit_cache=emit_cache),
        grid=(b, l // tm),
        in_specs=in_specs,
        out_specs=out_specs,
        out_shape=out_shape,
        compiler_params=_cparams(("parallel", "parallel")),
        name="mla_prep",
    )(*args)


def _mla_ctx_kernel(ckv_ref, kr_ref, wkn_ref, wv_ref, k_out, v_out):
    c16 = ckv_ref[0].astype(BF16)
    kn = _dot(c16, wkn_ref[...])
    v_out[0] = _dot(c16, wv_ref[...]).T.astype(v_out.dtype)
    kr16 = kr_ref[0].astype(k_out.dtype)
    zeros = jnp.zeros((kr16.shape[0], LANES - MLA_ROPE), k_out.dtype)
    for h in range(MLA_HEADS):
        k_out[0, :, h * 256:h * 256 + 128] = kn[:, h * 128:(h + 1) * 128].astype(k_out.dtype)
        k_out[0, :, h * 256 + 128:h * 256 + 128 + MLA_ROPE] = kr16
        k_out[0, :, h * 256 + 128 + MLA_ROPE:(h + 1) * 256] = zeros


def _mla_ctx(cache_ckv, cache_krope, wkn, wv, layer):
    b, _, p, _ = cache_ckv.shape
    return pl.pallas_call(
        _mla_ctx_kernel,
        grid=(b,),
        in_specs=[
            pl.BlockSpec((1, None, p, KV_LORA), lambda bb: (bb, layer, 0, 0)),
            pl.BlockSpec((1, None, p, MLA_ROPE), lambda bb: (bb, layer, 0, 0)),
            pl.BlockSpec((None, KV_LORA, 512), lambda bb: (layer, 0, 0)),
            pl.BlockSpec((None, KV_LORA, 512), lambda bb: (layer, 0, 0)),
        ],
        out_specs=[pl.BlockSpec((1, p, 1024), lambda bb: (bb, 0, 0)),
                   pl.BlockSpec((1, 512, p), lambda bb: (bb, 0, 0))],
        out_shape=[jax.ShapeDtypeStruct((b, p, 1024), BF16),
                   jax.ShapeDtypeStruct((b, 512, p), BF16)],
        compiler_params=_cparams(("parallel",)),
        name="mla_ctx",
    )(cache_ckv, cache_krope, wkn, wv)


def _attn_kernel(*refs, has_ctx, nk):
    q_ref = refs[0]
    if has_ctx:
        kc_ref, vc_ref, k_ref, v_ref, o_ref, m_ref, l_ref, acc_ref = refs[1:]
    else:
        k_ref, v_ref, o_ref, m_ref, l_ref, acc_ref = refs[1:]
    kk = pl.program_id(2)

    @pl.when(kk == 0)
    def _():
        m_ref[...] = jnp.full(m_ref.shape, -jnp.inf, F32)
        l_ref[...] = jnp.zeros(l_ref.shape, F32)
        acc_ref[...] = jnp.zeros(acc_ref.shape, F32)

    def body(kr, vtr):
        heads = range(MLA_HEADS)
        st = [_dot_nt(kr[0, :, h * 256:(h + 1) * 256], q_ref[0, :, h * 256:(h + 1) * 256]) for h in heads]
        m_prev = [m_ref[h] for h in heads]
        m_new = [jnp.maximum(m_prev[h], jnp.max(st[h], 0, keepdims=True)) for h in heads]
        alpha = [jnp.exp(m_prev[h] - m_new[h]) for h in heads]
        p = [jnp.exp(st[h] - m_new[h]) for h in heads]
        for h in heads:
            l_ref[h] = alpha[h] * l_ref[h] + jnp.sum(p[h], 0, keepdims=True)
            m_ref[h] = m_new[h]
        for h in heads:
            rows = slice(h * MLA_V, (h + 1) * MLA_V)
            acc_ref[rows, :] = alpha[h] * acc_ref[rows, :] + _dot(vtr[0, rows, :], p[h].astype(BF16))

    if has_ctx:
        @pl.when(kk == 0)
        def _():
            body(kc_ref, vc_ref)

        @pl.when(kk > 0)
        def _():
            body(k_ref, v_ref)
    else:
        body(k_ref, v_ref)

    @pl.when(kk == nk - 1)
    def _():
        for h in range(MLA_HEADS):
            hc = slice(h * MLA_V, (h + 1) * MLA_V)
            o_ref[0, :, hc] = (acc_ref[hc, :] / l_ref[h]).T.astype(o_ref.dtype)


def _attention(q, k, vt, ctx_kv, tq, tk):
    b, l, _ = q.shape
    has_ctx = ctx_kv is not None
    n_lat = l // tk
    nk = n_lat + (1 if has_ctx else 0)
    in_specs = [pl.BlockSpec((1, tq, 1024), lambda bb, i, j: (bb, i, 0))]
    args = [q]
    if has_ctx:
        assert ctx_kv[0].shape[1] == tk
        in_specs += [pl.BlockSpec((1, tk, 1024), lambda bb, i, j: (bb, 0, 0)),
                     pl.BlockSpec((1, 512, tk), lambda bb, i, j: (bb, 0, 0))]
        args += list(ctx_kv)
        kv_blk = lambda j: jnp.maximum(j - 1, 0)
    else:
        kv_blk = lambda j: j
    in_specs += [pl.BlockSpec((1, tk, 1024), lambda bb, i, j: (bb, kv_blk(j), 0)),
                 pl.BlockSpec((1, 512, tk), lambda bb, i, j: (bb, 0, kv_blk(j)))]
    args += [k, vt]
    return pl.pallas_call(
        functools.partial(_attn_kernel, has_ctx=has_ctx, nk=nk),
        grid=(b, l // tq, nk),
        in_specs=in_specs,
        out_specs=pl.BlockSpec((1, tq, 512), lambda bb, i, j: (bb, i, 0)),
        out_shape=jax.ShapeDtypeStruct((b, l, 512), BF16),
        scratch_shapes=[
            pltpu.VMEM((MLA_HEADS, 1, tq), F32),
            pltpu.VMEM((MLA_HEADS, 1, tq), F32),
            pltpu.VMEM((MLA_HEADS * MLA_V, tq), F32),
        ],
        compiler_params=_cparams(("parallel", "parallel", "arbitrary")),
        name="attn",
    )(*args)


def _merge_kernel(oa_ref, of_ref, ob_ref, dz_ref, oc_ref, gate_ref, x_ref, mod_ref, dng_ref,
                  wpa_ref, wpb_ref, wpc_ref, wout_ref, lng_ref, lnb_ref, o_ref):
    dng = dng_ref[...]
    s = of_ref[0].astype(F32) + ob_ref[0].astype(F32)
    dz = dz_ref[0].astype(F32)
    parts = []
    for h in range(DN_HEADS):
        hc = slice(h * DN_DV, (h + 1) * DN_DV)
        parts.append((_rms(s[:, hc], dng) * jax.nn.silu(dz[:, hc])).astype(BF16))
    o_dn = jnp.concatenate(parts, axis=1)
    gates = gate_ref[0]
    ga = jax.nn.sigmoid(gates[:, 0:D_MODEL].astype(F32))
    merged = ga * _dot(oa_ref[0], wpa_ref[...])
    gb = jax.nn.sigmoid(gates[:, D_MODEL:2 * D_MODEL].astype(F32))
    merged = merged + gb * _dot(o_dn, wpb_ref[...])
    gc = jax.nn.sigmoid(gates[:, 2 * D_MODEL:3 * D_MODEL].astype(F32))
    merged = merged + gc * _dot(oc_ref[0], wpc_ref[...])
    mix = _dot(merged.astype(BF16), wout_ref[...])
    g1 = mod_ref[0, 2:3, :]
    r = DN_ALPHA * x_ref[0] + g1 * mix
    o_ref[0] = _layer_norm(r, lng_ref[...], lnb_ref[...])


def _merge(o_a, o_f, o_b, y, o_c, x, mod, dn_norm, w_pa, w_pb, w_pc, w_out, ln_g, ln_b, layer):
    bm, lm, _ = x.shape
    tm = 512
    tok = lambda w, blk: pl.BlockSpec((1, tm, w), lambda b, i: (b, i, blk))
    wsp = lambda shp: pl.BlockSpec((None,) + shp, lambda b, i: (layer, 0, 0))
    return pl.pallas_call(
        _merge_kernel,
        grid=(bm, lm // tm),
        in_specs=[
            tok(512, 0), tok(512, 0), tok(512, 0), tok(512, DZ0 // 512), tok(512, 0),
            tok(3 * D_MODEL, 0), tok(D_MODEL, 0),
            pl.BlockSpec((1, 6, D_MODEL), lambda b, i: (b, 0, 0)),
            wsp((1, DN_DV)),
            wsp((A_WIDTH, D_MODEL)), wsp((DN_VW, D_MODEL)), wsp((512, D_MODEL)), wsp((D_MODEL, D_MODEL)),
            pl.BlockSpec((None, None, 1, D_MODEL), lambda b, i: (layer, 0, 0, 0)),
            pl.BlockSpec((None, None, 1, D_MODEL), lambda b, i: (layer, 0, 0, 0)),
        ],
        out_specs=tok(D_MODEL, 0),
        out_shape=jax.ShapeDtypeStruct((bm, lm, D_MODEL), F32),
        compiler_params=_cparams(("parallel", "parallel")),
        name="merge",
    )(o_a, o_f, o_b, y, o_c, y, x, mod, dn_norm, w_pa, w_pb, w_pc, w_out, ln_g, ln_b)


def _ffn_kernel(x_ref, mod_ref, wg_ref, wu_ref, wd_ref, lng_ref, lnb_ref, o_ref, h_ref, acc_ref, *, nf):
    f = pl.program_id(2)

    @pl.when(f == 0)
    def _():
        h_ref[...] = (x_ref[0] * (1.0 + mod_ref[0, 4:5, :]) + mod_ref[0, 3:4, :]).astype(BF16)
        acc_ref[...] = jnp.zeros(acc_ref.shape, F32)

    h = h_ref[...]
    act = (jax.nn.silu(_dot(h, wg_ref[...])) * _dot(h, wu_ref[...])).astype(BF16)
    acc_ref[...] += _dot(act, wd_ref[...])

    @pl.when(f == nf - 1)
    def _():
        r = DN_ALPHA * x_ref[0] + mod_ref[0, 5:6, :] * acc_ref[...]
        o_ref[0] = _layer_norm(r, lng_ref[...], lnb_ref[...])


def _ffn(x, mod, w_gu, w_down, ln_g, ln_b, layer, idx):
    bm, lm, _ = x.shape
    tm, tf = 1024, 256
    nf = D_FF // tf
    return pl.pallas_call(
        functools.partial(_ffn_kernel, nf=nf),
        grid=(bm, lm // tm, nf),
        in_specs=[
            pl.BlockSpec((1, tm, D_MODEL), lambda b, i, f: (b, i, 0)),
            pl.BlockSpec((1, 6, D_MODEL), lambda b, i, f: (b, 0, 0)),
            pl.BlockSpec((None, D_MODEL, tf), lambda b, i, f: (idx, 0, f)),
            pl.BlockSpec((None, D_MODEL, tf), lambda b, i, f: (idx, 0, nf + f)),
            pl.BlockSpec((None, tf, D_MODEL), lambda b, i, f: (idx, f, 0)),
            pl.BlockSpec((None, None, 1, D_MODEL), lambda b, i, f: (layer, 1, 0, 0)),
            pl.BlockSpec((None, None, 1, D_MODEL), lambda b, i, f: (layer, 1, 0, 0)),
        ],
        out_specs=pl.BlockSpec((1, tm, D_MODEL), lambda b, i, f: (b, i, 0)),
        out_shape=jax.ShapeDtypeStruct((bm, lm, D_MODEL), F32),
        scratch_shapes=[pltpu.VMEM((tm, D_MODEL), BF16), pltpu.VMEM((tm, D_MODEL), F32)],
        compiler_params=_cparams(("parallel", "parallel", "arbitrary")),
        name="ffn",
    )(x, mod, w_gu, w_gu, w_down, ln_g, ln_b)


def _route(h, wr, br):
    logits = _dot_hi(h, wr)
    lane = lax.broadcasted_iota(jnp.int32, logits.shape, 1)
    valid = lane < N_EXPERTS
    neg = -jnp.inf
    sel = jnp.where(valid, logits + br, neg)
    m1 = jnp.max(sel, -1, keepdims=True)
    i1 = jnp.min(jnp.where(sel == m1, lane, LANES), -1, keepdims=True)
    sel2 = jnp.where(lane == i1, neg, sel)
    m2 = jnp.max(sel2, -1, keepdims=True)
    i2 = jnp.min(jnp.where(sel2 == m2, lane, LANES), -1, keepdims=True)
    l1 = jnp.sum(jnp.where(lane == i1, logits, 0.0), -1, keepdims=True)
    l2 = jnp.sum(jnp.where(lane == i2, logits, 0.0), -1, keepdims=True)
    mx = jnp.maximum(l1, l2)
    e1 = jnp.exp(l1 - mx)
    e2 = jnp.exp(l2 - mx)
    den = e1 + e2
    comb = jnp.where(lane == i1, e1 / den, 0.0) + jnp.where(lane == i2, e2 / den, 0.0)
    return comb, (lane == i1) | (lane == i2)


MOE_TM = 1024
MOE_CAP0 = 256
MOE_TAIL = 32
MOE_CAP1 = 256


def _moe_kernel(x_ref, mod_ref, wr_ref, br_ref, tri_ref, wg_ref, wu_ref, wd_ref, lng_ref, lnb_ref, o_ref,
                h_ref, comb_ref, slot_ref, slot_t_ref, acc_ref, ytail_ref):
    e = pl.program_id(2)
    tm = h_ref.shape[0]
    always = MOE_CAP0 + MOE_TAIL

    @pl.when(e == 0)
    def _():
        hf = x_ref[0] * (1.0 + mod_ref[0, 4:5, :]) + mod_ref[0, 3:4, :]
        h_ref[...] = hf.astype(BF16)
        comb, sel = _route(hf, wr_ref[...], br_ref[...])
        comb_ref[...] = comb
        sel16 = jnp.where(sel, 1.0, 0.0).astype(BF16)
        nb = tri_ref.shape[0]
        carry = jnp.zeros((1, LANES), F32)
        for blk in range(tm // nb):
            rows = slice(blk * nb, (blk + 1) * nb)
            rank = _dot(tri_ref[...], sel16[rows]) + carry
            slot_ref[rows, :] = jnp.where(sel[rows], rank, -1.0)
            carry = carry + jnp.sum(sel16[rows].astype(F32), 0, keepdims=True)
        slot_t_ref[...] = slot_ref[...].T
        acc_ref[...] = jnp.zeros(acc_ref.shape, F32)

    lane = lax.broadcasted_iota(jnp.int32, comb_ref.shape, 1)
    pick = lane == e
    c_col = jnp.sum(jnp.where(pick, comb_ref[...], 0.0), -1, keepdims=True)
    slot_col = jnp.max(jnp.where(pick, slot_ref[...], -1.0), -1, keepdims=True)
    slot_row = slot_t_ref[pl.ds(e, 1), :]
    load = jnp.max(slot_row) + 1.0

    def expert(base, cap):
        r_i = lax.broadcasted_iota(jnp.int32, (cap, tm), 0).astype(F32) + base
        gather = jnp.where(r_i == slot_row, 1.0, 0.0).astype(BF16)
        xe = _dot(gather, h_ref[...]).astype(BF16)
        act = (jax.nn.silu(_dot(xe, wg_ref[...])) * _dot(xe, wu_ref[...])).astype(BF16)
        return gather, _dot(act, wd_ref[...])

    def scatter(base, ye16):
        c_i = lax.broadcasted_iota(jnp.int32, (tm, ye16.shape[0]), 1).astype(F32) + base
        onehot = jnp.where(c_i == slot_col, 1.0, 0.0).astype(BF16)
        acc_ref[...] += c_col * _dot(onehot, ye16)

    def block(base, cap):
        _, ye = expert(base, cap)
        scatter(base, ye.astype(BF16))

    gather, ye = expert(0.0, always)
    scatter(0.0, ye[:MOE_CAP0].astype(BF16))
    c_b = jnp.broadcast_to(c_col, (tm, LANES))
    g_tail = gather[MOE_CAP0:]
    c_slot = sum(_dot(g_tail, piece) for piece in _split3(c_b))
    ytail_ref[pl.ds(pl.multiple_of(e * MOE_TAIL, MOE_TAIL), MOE_TAIL), :] = (
        ye[MOE_CAP0:] * c_slot[:, 0:1]).astype(BF16)
    base = always
    while base < tm:
        pl.when(load > base)(functools.partial(block, float(base), MOE_CAP1))
        base += MOE_CAP1

    @pl.when(e == N_EXPERTS - 1)
    def _():
        n_tail = N_EXPERTS * MOE_TAIL
        slot = slot_ref[...]
        in_tail = (slot >= MOE_CAP0) & (slot < always)
        tslot = jnp.where(in_tail, slot - MOE_CAP0, 2.0 * MOE_TAIL).astype(BF16)
        e_i = lax.broadcasted_iota(jnp.int32, (LANES, n_tail), 0)
        l_i = lax.broadcasted_iota(jnp.int32, (LANES, n_tail), 1)
        expand = jnp.where(l_i // MOE_TAIL == e_i, 1.0, 0.0).astype(BF16)
        want = (lax.broadcasted_iota(jnp.int32, (tm, n_tail), 1) % MOE_TAIL).astype(F32)
        onehot = jnp.where(_dot(tslot, expand) == want, 1.0, 0.0).astype(BF16)
        ff = acc_ref[...] + _dot(onehot, ytail_ref[...])
        r = DN_ALPHA * x_ref[0] + mod_ref[0, 5:6, :] * ff
        o_ref[0] = _layer_norm(r, lng_ref[...], lnb_ref[...])


def _moe(x, mod, w_router_p, b_router_p, tri, w_gu, w_down, ln_g, ln_b, layer, idx):
    bm, lm, _ = x.shape
    tm = MOE_TM
    nb = tri.shape[0]
    return pl.pallas_call(
        _moe_kernel,
        grid=(bm, lm // tm, N_EXPERTS),
        in_specs=[
            pl.BlockSpec((1, tm, D_MODEL), lambda b, i, e: (b, i, 0)),
            pl.BlockSpec((1, 6, D_MODEL), lambda b, i, e: (b, 0, 0)),
            pl.BlockSpec((None, D_MODEL, LANES), lambda b, i, e: (idx, 0, 0)),
            pl.BlockSpec((None, 1, LANES), lambda b, i, e: (idx, 0, 0)),
            pl.BlockSpec((nb, nb), lambda b, i, e: (0, 0)),
            pl.BlockSpec((None, None, D_MODEL, D_FF_EXPERT), lambda b, i, e: (idx, e, 0, 0)),
            pl.BlockSpec((None, None, D_MODEL, D_FF_EXPERT), lambda b, i, e: (idx, e, 0, 1)),
            pl.BlockSpec((None, None, D_FF_EXPERT, D_MODEL), lambda b, i, e: (idx, e, 0, 0)),
            pl.BlockSpec((None, None, 1, D_MODEL), lambda b, i, e: (layer, 1, 0, 0)),
            pl.BlockSpec((None, None, 1, D_MODEL), lambda b, i, e: (layer, 1, 0, 0)),
        ],
        out_specs=pl.BlockSpec((1, tm, D_MODEL), lambda b, i, e: (b, i, 0)),
        out_shape=jax.ShapeDtypeStruct((bm, lm, D_MODEL), F32),
        scratch_shapes=[pltpu.VMEM((tm, D_MODEL), BF16), pltpu.VMEM((tm, LANES), F32),
                        pltpu.VMEM((tm, LANES), F32), pltpu.VMEM((LANES, tm), F32),
                        pltpu.VMEM((tm, D_MODEL), F32), pltpu.VMEM((N_EXPERTS * MOE_TAIL, D_MODEL), BF16)],
        compiler_params=_cparams(("parallel", "parallel", "arbitrary")),
        name="moe",
    )(x, mod, w_router_p, b_router_p, tri, w_gu, w_gu, w_down, ln_g, ln_b)


def _pack_w_in(w_in):
    d = w_in.shape[0]
    w_in = w_in.astype(BF16)
    o = 0
    cols = {}
    for name, width in (("a_u", A_WIDTH), ("a_v", A_WIDTH), ("dq", DN_QK), ("dk", DN_QK), ("dv", DN_VW),
                        ("dz", DN_VW), ("sm", 4 * DN_HEADS), ("cq", Q_LORA), ("ckv", KV_LORA), ("kr", MLA_ROPE),
                        ("ga", D_MODEL), ("gb", D_MODEL), ("gc", D_MODEL)):
        cols[name] = w_in[:, :, o:o + width]
        o += width
    zpad = lambda n: jnp.zeros((d, D_MODEL, n), w_in.dtype)
    packed = jnp.concatenate(
        [cols["ga"], cols["gb"], cols["gc"], cols["dq"], cols["dk"], cols["dv"], cols["dz"],
         cols["a_u"], cols["a_v"], cols["cq"], cols["ckv"],
         cols["kr"], zpad(LANES - MLA_ROPE), cols["sm"], zpad(LANES - 4 * DN_HEADS)], axis=-1)
    return packed


def _rope_tables(n_tok):
    row = (jnp.arange(n_tok) // GRID_W).astype(F32)
    col = (jnp.arange(n_tok) % GRID_W).astype(F32)
    inv_freq = ROPE_BASE ** (-jnp.arange(AXIS_FREQS, dtype=F32) / AXIS_FREQS)
    ang_r = row[:, None] * inv_freq
    ang_c = col[:, None] * inv_freq
    ones = jnp.ones((n_tok, LANES - MLA_ROPE), F32)
    cos = jnp.concatenate([jnp.cos(ang_r), jnp.cos(ang_r), jnp.cos(ang_c), jnp.cos(ang_c), ones], axis=1)
    sin = jnp.concatenate([-jnp.sin(ang_r), jnp.sin(ang_r), -jnp.sin(ang_c), jnp.sin(ang_c), 0.0 * ones], axis=1)
    return cos, sin


def kernel(x_prompt, x_sample, state_dn, cache_ckv, cache_krope, c, c_ctx, w_mod, b_mod, w_in, a_ln_g, a_ln_b, a_ws, a_bs, dn_conv, dn_a_log, dn_dt_bias, dn_norm, q_norm, w_qb, kv_norm, w_kvb, w_pa, w_pb, w_pc, w_out, ln_g, ln_b, ffn_gu, ffn_down, moe_router, moe_bias, moe_gu, moe_down):
    batch, seq, _ = x_prompt.shape
    dec_batch, dec_seq, _ = x_sample.shape
    depth = w_in.shape[0]
    assert depth == DEPTH

    w_in_p = _pack_w_in(w_in)
    a_bs_t = jnp.swapaxes(a_bs, 1, 2)
    ln_a_g = a_ln_g.reshape(depth, 1, A_WIDTH)
    ln_a_b = a_ln_b.reshape(depth, 1, A_WIDTH)
    lane_pad = lambda v: jnp.pad(v.reshape(depth, 1, -1), ((0, 0), (0, 0), (0, LANES - v[0].size)))
    alog_row = lane_pad(dn_a_log)
    dtb_row = lane_pad(dn_dt_bias)
    dn_g = dn_norm.reshape(depth, 1, DN_DV)
    qg = q_norm.reshape(depth, 1, Q_LORA)
    kvg = kv_norm.reshape(depth, 1, KV_LORA)
    wq = w_qb.reshape(depth, Q_LORA, MLA_HEADS, MLA_NOPE + MLA_ROPE)
    wqn = wq[..., :MLA_NOPE].reshape(depth, Q_LORA, MLA_HEADS * MLA_NOPE).astype(BF16)
    wqr = jnp.pad(wq[..., MLA_NOPE:], ((0, 0), (0, 0), (0, 0), (0, LANES - MLA_ROPE)))
    wqr = wqr.reshape(depth, Q_LORA, MLA_HEADS * LANES).astype(BF16)
    wkv = w_kvb.reshape(depth, KV_LORA, MLA_HEADS, MLA_NOPE + MLA_V)
    wkn = wkv[..., :MLA_NOPE].reshape(depth, KV_LORA, MLA_HEADS * MLA_NOPE).astype(BF16)
    wv = wkv[..., MLA_NOPE:].reshape(depth, KV_LORA, MLA_HEADS * MLA_V).astype(BF16)
    w_pa16, w_pb16, w_pc16, w_out16 = (w.astype(BF16) for w in (w_pa, w_pb, w_pc, w_out))
    ffn_gu16, ffn_down16 = ffn_gu.astype(BF16), ffn_down.astype(BF16)
    moe_gu16, moe_down16 = moe_gu.astype(BF16), moe_down.astype(BF16)
    n_moe = moe_router.shape[0]
    wr_p = jnp.pad(moe_router, ((0, 0), (0, 0), (0, LANES - N_EXPERTS)))
    br_p = jnp.pad(moe_bias.reshape(n_moe, 1, N_EXPERTS), ((0, 0), (0, 0), (0, LANES - N_EXPERTS)))
    ln_g4 = ln_g.reshape(depth, 2, 1, D_MODEL)
    ln_b4 = ln_b.reshape(depth, 2, 1, D_MODEL)
    rope_tabs = _rope_tables(dec_seq)
    gdn_tabs = _gdn_tables(GDN_TILE)
    moe_tri = (jnp.arange(256)[None, :] < jnp.arange(256)[:, None]).astype(BF16)

    n_cond = 16
    cond = jnp.concatenate([c, c_ctx[None, :], jnp.zeros((n_cond - dec_batch - 1, D_MODEL), F32)], axis=0)
    mods = _adaln(cond, w_mod, b_mod).reshape(depth, n_cond, 6, D_MODEL)

    xp = x_prompt.reshape(1, batch * seq, D_MODEL)
    xs = x_sample
    st_dn, st_ckv, st_kr = [], [], []

    def channel_mixer(x, mod, l):
        if l % 2 == 0:
            return _ffn(x, mod, ffn_gu16, ffn_down16, ln_g4, ln_b4, l, l // 2)
        return _moe(x, mod, wr_p, br_p, moe_tri, moe_gu16, moe_down16, ln_g4, ln_b4, l, l // 2)

    def token_mixer(x, mod, l, bsz, n_tok, latent):
        y = _inproj(x, mod, w_in_p, l)
        o_a = _gmlp(y.reshape(-1, NP), a_ws, a_bs_t, ln_a_g, ln_a_b, l).reshape(x.shape[0], x.shape[1], A_WIDTH)
        ys = y.reshape(bsz, n_tok, NP)
        o_f, o_b, s_fin = _gdn(ys, dn_conv, alog_row, dtb_row, gdn_tabs, state_dn if latent else None, l)
        if latent:
            q, k, v = _mla_prep(ys, qg, kvg, wqn, wqr, wkn, wv, l, rope_tabs, False)
            ctx_kv = _mla_ctx(cache_ckv, cache_krope, wkn, wv, l)
            o_c = _attention(q, k, v, ctx_kv, 512, 512)
            extra = None
        else:
            q, k, v, ckv_n, kr = _mla_prep(ys, qg, kvg, wqn, wqr, wkn, wv, l, None, True)
            o_c = _attention(q, k, v, None, n_tok, n_tok)
            extra = (s_fin, ckv_n, kr)
        shp = (x.shape[0], x.shape[1], 512)
        x = _merge(o_a, o_f.reshape(shp), o_b.reshape(shp), y, o_c.reshape(shp), x, mod, dn_g,
                   w_pa16, w_pb16, w_pc16, w_out16, ln_g4, ln_b4, l)
        return x, extra

    for l in range(depth):
        mod_p = mods[l, dec_batch:dec_batch + 1]
        mod_s = mods[l, :dec_batch]
        xp, (s_fin, ckv_n, kr) = token_mixer(xp, mod_p, l, batch, seq, False)
        xp = channel_mixer(xp, mod_p, l)
        st_dn.append(s_fin)
        st_ckv.append(ckv_n)
        st_kr.append(kr)
        xs, _ = token_mixer(xs, mod_s, l, dec_batch, dec_seq, True)
        xs = channel_mixer(xs, mod_s, l)

    return (xp.reshape(batch, seq, D_MODEL), xs,
            jnp.stack(st_dn, axis=1), jnp.stack(st_ckv, axis=1), jnp.stack(st_kr, axis=1))
```

```python
import functools
import math

import jax
import jax.numpy as jnp
from jax import lax
from jax.experimental import pallas as pl
from jax.experimental.pallas import tpu as pltpu

F32 = jnp.float32
BF16 = jnp.bfloat16

D_MODEL = 1024
DEPTH = 4
GRID_W = 64
A_WIDTH = 512
A_CHUNK = 128
A_GROUPS = 4
DN_HEADS = 4
DN_DK = 128
DN_DV = 128
DN_QK = DN_HEADS * DN_DK
DN_VW = DN_HEADS * DN_DV
DN_CONV_K = 5
GDN_TILE = 256
MLA_HEADS = 4
MLA_NOPE = 128
MLA_ROPE = 64
MLA_V = 128
Q_LORA = 256
KV_LORA = 256
AXIS_FREQS = MLA_ROPE // 4
ROPE_BASE = 10000.0
D_FF = 2816
N_EXPERTS = 8
D_FF_EXPERT = D_FF // 2
DN_ALPHA = (2 * DEPTH) ** 0.25
EPS = 1e-6
LN_EPS = 1e-5

LANES = 128
HALO = 16
VMEM_LIMIT = 56 * 1024 * 1024

GATE0 = 0
QKV0 = 3 * D_MODEL
DZ0 = QKV0 + 3 * DN_QK
AU0 = DZ0 + DN_VW
AV0 = AU0 + A_WIDTH
CQ0 = AV0 + A_WIDTH
CKV0 = CQ0 + Q_LORA
KR0 = CKV0 + KV_LORA
SM0 = KR0 + LANES
NP = SM0 + LANES
QKV_W = 3 * DN_QK


def _cparams(sem):
    return pltpu.CompilerParams(dimension_semantics=sem, vmem_limit_bytes=VMEM_LIMIT)


def _dot(a, b):
    return jnp.dot(a, b, preferred_element_type=F32)


def _dot_nt(a, b):
    return lax.dot_general(a, b, (((1,), (1,)), ((), ())), preferred_element_type=F32)


def _dot_tn(a, b):
    return lax.dot_general(a, b, (((0,), (0,)), ((), ())), preferred_element_type=F32)


def _split3(x):
    hi = x.astype(BF16)
    r = x - hi.astype(F32)
    mid = r.astype(BF16)
    lo = (r - mid.astype(F32)).astype(BF16)
    return hi, mid, lo


def _dot_f32(a, b):
    ah, am, al = _split3(a)
    bh, bm, bl = _split3(b)
    return (_dot(ah, bh) + (_dot(ah, bm) + _dot(am, bh))
            + (_dot(am, bm) + _dot(ah, bl) + _dot(al, bh)))


def _dot_hi(a, b):
    ah, am, _ = _split3(a)
    bh, bm, _ = _split3(b)
    return _dot(ah, bh) + (_dot(ah, bm) + _dot(am, bh))


def _layer_norm(r, g, b):
    mu = jnp.mean(r, -1, keepdims=True)
    rc = r - mu
    var = jnp.mean(rc * rc, -1, keepdims=True)
    return rc * lax.rsqrt(var + LN_EPS) * g + b


def _adaln_kernel(c_ref, w_ref, b_ref, o_ref):
    s = jax.nn.silu(c_ref[...])
    o_ref[...] = _dot_f32(s, w_ref[...]) + b_ref[...]


def _adaln(cond, w_mod, b_mod):
    rows = cond.shape[0]
    n_out = w_mod.shape[-1]
    tn = 1536
    return pl.pallas_call(
        _adaln_kernel,
        grid=(DEPTH, n_out // tn),
        in_specs=[
            pl.BlockSpec((rows, D_MODEL), lambda l, j: (0, 0)),
            pl.BlockSpec((None, D_MODEL, tn), lambda l, j: (l, 0, j)),
            pl.BlockSpec((None, 1, tn), lambda l, j: (l, 0, j)),
        ],
        out_specs=pl.BlockSpec((None, rows, tn), lambda l, j: (l, 0, j)),
        out_shape=jax.ShapeDtypeStruct((DEPTH, rows, n_out), F32),
        compiler_params=_cparams(("parallel", "parallel")),
        name="adaln",
    )(cond, w_mod, b_mod.reshape(DEPTH, 1, n_out))


def _inproj_kernel(x_ref, mod_ref, w_ref, y_ref, h_ref):
    @pl.when(pl.program_id(2) == 0)
    def _():
        sh = mod_ref[0, 0:1, :]
        sc = mod_ref[0, 1:2, :]
        h_ref[...] = (x_ref[0] * (1.0 + sc) + sh).astype(BF16)

    y_ref[0] = _dot(h_ref[...], w_ref[...]).astype(y_ref.dtype)


def _inproj(x, mod, w_in_p, layer):
    bm, lm, _ = x.shape
    tm, tn = 1024, 1152
    return pl.pallas_call(
        _inproj_kernel,
        grid=(bm, lm // tm, NP // tn),
        in_specs=[
            pl.BlockSpec((1, tm, D_MODEL), lambda b, i, j: (b, i, 0)),
            pl.BlockSpec((1, 6, D_MODEL), lambda b, i, j: (b, 0, 0)),
            pl.BlockSpec((None, D_MODEL, tn), lambda b, i, j: (layer, 0, j)),
        ],
        out_specs=pl.BlockSpec((1, tm, tn), lambda b, i, j: (b, i, j)),
        out_shape=jax.ShapeDtypeStruct((bm, lm, NP), BF16),
        scratch_shapes=[pltpu.VMEM((tm, D_MODEL), BF16)],
        compiler_params=_cparams(("parallel", "parallel", "arbitrary")),
        name="inproj",
    )(x, mod, w_in_p)


def _gmlp_kernel(u_ref, v_ref, ws_ref, bs_ref, g_ref, b_ref, o_ref):
    tl = u_ref.shape[0]
    gd = A_WIDTH // A_GROUPS
    v = jax.nn.gelu(v_ref[...].astype(F32))
    vn = _layer_norm(v, g_ref[...], b_ref[...]).astype(BF16)
    u = jax.nn.gelu(u_ref[...].astype(F32))
    for n in range(tl // A_CHUNK):
        rows = slice(n * A_CHUNK, (n + 1) * A_CHUNK)
        for g in range(A_GROUPS):
            cols = slice(g * gd, (g + 1) * gd)
            mixed = _dot(ws_ref[g].astype(BF16), vn[rows, cols]) + bs_ref[:, g:g + 1]
            o_ref[rows, cols] = (u[rows, cols] * mixed).astype(o_ref.dtype)


def _gmlp(y2d, a_ws, a_bs_t, ln_g, ln_b, layer):
    t = y2d.shape[0]
    tl = 512
    return pl.pallas_call(
        _gmlp_kernel,
        grid=(t // tl,),
        in_specs=[
            pl.BlockSpec((tl, A_WIDTH), lambda i: (i, AU0 // A_WIDTH)),
            pl.BlockSpec((tl, A_WIDTH), lambda i: (i, AV0 // A_WIDTH)),
            pl.BlockSpec((None, A_GROUPS, A_CHUNK, A_CHUNK), lambda i: (layer, 0, 0, 0)),
            pl.BlockSpec((None, A_CHUNK, A_GROUPS), lambda i: (layer, 0, 0)),
            pl.BlockSpec((None, 1, A_WIDTH), lambda i: (layer, 0, 0)),
            pl.BlockSpec((None, 1, A_WIDTH), lambda i: (layer, 0, 0)),
        ],
        out_specs=pl.BlockSpec((tl, A_WIDTH), lambda i: (i, 0)),
        out_shape=jax.ShapeDtypeStruct((t, A_WIDTH), BF16),
        compiler_params=_cparams(("parallel",)),
        name="gmlp",
    )(y2d, y2d, a_ws, a_bs_t, ln_g, ln_b)


GDN_SCRATCH = {"kb": DN_DK, "k": DN_DK, "q": DN_DK, "qe": DN_DK, "kdec": DN_DK, "rhs": DN_DV + DN_DK,
               "low": None, "qk": None, "inv": None}


def _gdn_stage1(d, reverse, is_first, is_last, qkv_ref, prev_ref, next_ref, sm_ref, conv_ref, alog_ref,
                dtb_ref, tri_ref, lvl_ref, ext_ref, act_ref, tile, conv_now, o_ref, tl, sc, probs):
    c = tl
    off = HALO - DN_CONV_K // 2
    st = {}

    def fill():
        ext_ref[0:HALO, :] = jnp.where(is_first, 0.0, prev_ref[0].astype(F32))
        ext_ref[HALO:HALO + tl, :] = qkv_ref[0].astype(F32)
        ext_ref[HALO + tl:2 * HALO + tl, :] = jnp.where(is_last, 0.0, next_ref[0].astype(F32))

    rb = 64

    def conv(part, r0):
        cols = slice(part * DN_QK, (part + 1) * DN_QK)
        acc = conv_ref[0:1, cols] * ext_ref[off + r0:off + r0 + rb, cols]
        for j in range(1, DN_CONV_K):
            acc = acc + conv_ref[j:j + 1, cols] * ext_ref[off + j + r0:off + j + r0 + rb, cols]
        y = jax.nn.silu(acc)
        if part < 2:
            for h in range(DN_HEADS):
                yh = y[:, h * DN_DK:(h + 1) * DN_DK]
                yh = yh * lax.rsqrt(jnp.sum(yh * yh, -1, keepdims=True) + EPS)
                act_ref[tile, r0:r0 + rb, part * DN_QK + h * DN_DK:part * DN_QK + (h + 1) * DN_DK] = (
                    yh.astype(act_ref.dtype))
        else:
            act_ref[tile, r0:r0 + rb, cols] = y.astype(act_ref.dtype)

    def conv_all():
        fill()
        for part in range(3):
            for r0 in range(0, tl, rb):
                conv(part, r0)

    def maybe_conv():
        if conv_now is not None:
            pl.when(conv_now)(conv_all)

    def gates():
        sm = sm_ref[0].astype(F32)
        g_all = -jnp.exp(alog_ref[...]) * jax.nn.softplus(sm + dtb_ref[...])
        st["beta"] = jax.nn.sigmoid(sm)
        tri = tri_ref[d]
        gh, gm, gl = _split3(g_all)
        st["gam"] = _dot(tri, gh) + _dot(tri, gm) + _dot(tri, gl)
        st["gam_t"] = st["gam"].T
        ii = lax.broadcasted_iota(jnp.int32, (c, c), 0)
        jj = lax.broadcasted_iota(jnp.int32, (c, c), 1)
        st["incl"] = jnp.where((jj >= ii) if reverse else (jj <= ii), 1.0, 0.0).astype(F32)
        st["strict"] = jnp.where((jj > ii) if reverse else (jj < ii), 1.0, 0.0).astype(F32)
        st["eye"] = jnp.where(ii == jj, 1.0, 0.0).astype(F32)

    def head(h):
        scale = DN_DK ** -0.5
        lane = d * DN_HEADS + h
        pi = lane
        gam, gam_t, beta = st["gam"], st["gam_t"], st["beta"]
        gr = gam_t[lane:lane + 1, :]
        g_last = gam[0:1, lane:lane + 1] if reverse else gam[c - 1:c, lane:lane + 1]
        ob = 64
        for r0 in range(0, c, ob):
            rows = slice(r0, r0 + ob)
            gc = gam[rows, lane:lane + 1]
            bcol = beta[rows, 2 * DN_HEADS + lane:2 * DN_HEADS + lane + 1]
            eg = jnp.exp(gc)
            qh = act_ref[tile, rows, h * DN_DK:(h + 1) * DN_DK].astype(F32) * scale
            kh = act_ref[tile, rows, DN_QK + h * DN_DK:DN_QK + (h + 1) * DN_DK].astype(F32)
            vh = act_ref[tile, rows, 2 * DN_QK + h * DN_DV:2 * DN_QK + (h + 1) * DN_DV].astype(F32)
            kb = kh * bcol
            sc["kb"][pi, rows, :] = kb.astype(BF16)
            sc["k"][pi, rows, :] = kh.astype(BF16)
            sc["q"][pi, rows, :] = qh.astype(BF16)
            sc["qe"][pi, rows, :] = (qh * eg).astype(BF16)
            sc["kdec"][pi, rows, :] = (kh * jnp.exp(g_last - gc)).astype(BF16)
            sc["rhs"][pi, rows, 0:DN_DV] = (vh * bcol).astype(BF16)
            sc["rhs"][pi, rows, DN_DV:2 * DN_DV] = (kb * eg).astype(BF16)
        kk = _dot_nt(sc["kb"][pi], sc["k"][pi])
        qk = _dot_nt(sc["q"][pi], sc["k"][pi])
        mb = 32
        for r0 in range(0, c, mb):
            rows = slice(r0, r0 + mb)
            gc = gam[rows, lane:lane + 1]
            dec = jnp.exp(jnp.minimum(gc - gr, 0.0))
            low = kk[rows] * dec * st["strict"][rows]
            sc["qk"][pi, rows, :] = (qk[rows] * dec * st["incl"][rows]).astype(BF16)
            sc["low"][pi, rows, :] = low.astype(BF16)
            sc["inv"][pi, rows, :] = (st["eye"][rows] - low * lvl_ref[d, 0, rows, :]).astype(BF16)
        probs.append(dict(d=d, h=h, pi=pi, o_ref=o_ref, e_last=jnp.exp(g_last)))

    return [maybe_conv, gates] + [functools.partial(head, h) for h in range(DN_HEADS)]


def _gdn_level(probs, lvl_ref, sc, lv):
    for p in probs:
        p["t16"] = _dot(sc["low"][p["pi"]], sc["inv"][p["pi"]]).astype(BF16)
    for p in probs:
        x16 = _dot(sc["inv"][p["pi"]], p["t16"]).astype(BF16)
        sc["inv"][p["pi"]] = sc["inv"][p["pi"]] - x16 * lvl_ref[p["d"], lv]


GDN_SUB = 64


def _gdn_solve_all(probs, sc, c):
    nb = c // GDN_SUB
    xs = {p["pi"]: [None] * nb for p in probs}
    col = lax.broadcasted_iota(jnp.int32, (GDN_SUB, c), 1)
    for step in range(nb):
        todo = []
        for p in probs:
            pi, rev = p["pi"], p["d"] == 1
            b = nb - 1 - step if rev else step
            rows = slice(b * GDN_SUB, (b + 1) * GDN_SUB)
            todo.append((pi, b, rows))
            if step > 0:
                keep = (col >= (b + 1) * GDN_SUB) if rev else (col < b * GDN_SUB)
                loff = sc["low"][pi, rows, :] * jnp.where(keep, 1.0, 0.0).astype(BF16)
                rr = sc["rhs"][pi, rows, :].astype(F32) - _dot(loff, sc["rhs"][pi])
                sc["rhs"][pi, rows, :] = rr.astype(BF16)
        for pi, b, rows in todo:
            x = _dot(sc["inv"][pi, rows, :], sc["rhs"][pi])
            xs[pi][b] = x
            sc["rhs"][pi, rows, :] = x.astype(BF16)
    for p in probs:
        sol = jnp.concatenate(xs[p["pi"]], axis=0)
        p["u"] = sol[:, :DN_DV]
        p["w16"] = sol[:, DN_DV:].astype(BF16)


def _gdn_scan(p, s_ref, sc):
    st = s_ref[p["d"], p["h"]]
    st16 = st.astype(BF16)
    v16 = (p["u"] - _dot(p["w16"], st16)).astype(BF16)
    o = _dot(sc["qe"][p["pi"]], st16) + _dot(sc["qk"][p["pi"]], v16)
    s_ref[p["d"], p["h"]] = st * p["e_last"] + _dot_tn(sc["kdec"][p["pi"]], v16)
    p["o_ref"][0, :, p["h"] * DN_DV:(p["h"] + 1) * DN_DV] = o.astype(p["o_ref"].dtype)


def _gdn_kernel(*refs, tl, n_t, has_init):
    (qkv_f, prev_f, next_f, sm_f, qkv_b, prev_b, next_b, sm_b,
     conv_ref, alog_ref, dtb_ref, tri_ref, lvl_ref) = refs[:13]
    k = 13
    if has_init:
        s0_ref = refs[k]
        k += 1
    o_f, o_b, sfin_ref, ext_f, ext_b, act_c, s_ref = refs[k:k + 7]
    sc = dict(zip(GDN_SCRATCH, refs[k + 7:]))
    i = pl.program_id(1)
    n_lvl = lvl_ref.shape[1]
    first_visit = i < (n_t + 1) // 2
    conv_b = None if n_t == 1 else first_visit

    @pl.when(i == 0)
    def _():
        if has_init:
            s_ref[...] = s0_ref[0]
        else:
            s_ref[...] = jnp.zeros(s_ref.shape, F32)

    probs_f, probs_b = [], []
    items_f = _gdn_stage1(0, False, i == 0, i == n_t - 1, qkv_f, prev_f, next_f, sm_f, conv_ref, alog_ref,
                          dtb_ref, tri_ref, lvl_ref, ext_f, act_c, i, first_visit, o_f, tl, sc, probs_f)
    items_b = _gdn_stage1(1, True, i == n_t - 1, i == 0, qkv_b, prev_b, next_b, sm_b, conv_ref, alog_ref,
                          dtb_ref, tri_ref, lvl_ref, ext_b, act_c, n_t - 1 - i, conv_b, o_b, tl, sc, probs_b)
    for item in items_f + items_b:
        item()
    probs = probs_f + probs_b
    for lv in range(1, GDN_SUB.bit_length() - 1):
        _gdn_level(probs, lvl_ref, sc, lv)
    _gdn_solve_all(probs, sc, tl)
    for p in probs:
        _gdn_scan(p, s_ref, sc)

    @pl.when(i == n_t - 1)
    def _():
        sfin_ref[0] = s_ref[...]


def _gdn_tables(c):
    ii = jnp.arange(c)[:, None]
    jj = jnp.arange(c)[None, :]
    tri = jnp.stack([jj <= ii, jj >= ii]).astype(BF16)
    x = ii ^ jj
    lvls = []
    s = 1
    while s < c:
        pair = (x >= s) & (x < 2 * s)
        lvls.append(jnp.stack([pair & (jj < ii), pair & (jj > ii)]))
        s *= 2
    return tri, jnp.stack(lvls, axis=1).astype(BF16)


def _gdn(y, conv_w, alog_row, dtb_row, tables, s0, layer):
    b, l, _ = y.shape
    tl = GDN_TILE
    n_t = l // tl
    assert n_t == 1 or n_t % 2 == 0
    hb = tl // HALO
    n_hb = l // HALO
    qkv_blk = QKV0 // QKV_W
    sm_blk = SM0 // LANES
    has_init = s0 is not None

    def fwd(i):
        return i

    def bwd(i):
        return n_t - 1 - i

    def tile_specs(pos):
        return [
            pl.BlockSpec((1, tl, QKV_W), lambda bb, i: (bb, pos(i), qkv_blk)),
            pl.BlockSpec((1, HALO, QKV_W), lambda bb, i: (bb, jnp.maximum(pos(i) * hb - 1, 0), qkv_blk)),
            pl.BlockSpec((1, HALO, QKV_W), lambda bb, i: (bb, jnp.minimum((pos(i) + 1) * hb, n_hb - 1), qkv_blk)),
            pl.BlockSpec((1, tl, LANES), lambda bb, i: (bb, pos(i), sm_blk)),
        ]

    tri, lvls = tables
    n_lvl = lvls.shape[1]
    in_specs = tile_specs(fwd) + tile_specs(bwd) + [
        pl.BlockSpec((None, DN_CONV_K, QKV_W), lambda bb, i: (layer, 0, 0)),
        pl.BlockSpec((None, 1, LANES), lambda bb, i: (layer, 0, 0)),
        pl.BlockSpec((None, 1, LANES), lambda bb, i: (layer, 0, 0)),
        pl.BlockSpec((2, tl, tl), lambda bb, i: (0, 0, 0)),
        pl.BlockSpec((2, n_lvl, tl, tl), lambda bb, i: (0, 0, 0, 0)),
    ]
    args = [y] * 8 + [conv_w, alog_row, dtb_row, tri, lvls]
    if has_init:
        in_specs.append(pl.BlockSpec((1, None, 2, DN_HEADS, DN_DK, DN_DV), lambda bb, i: (bb, layer, 0, 0, 0, 0)))
        args.append(s0)
    return pl.pallas_call(
        functools.partial(_gdn_kernel, tl=tl, n_t=n_t, has_init=has_init),
        grid=(b, n_t),
        in_specs=in_specs,
        out_specs=[
            pl.BlockSpec((1, tl, DN_VW), lambda bb, i: (bb, i, 0)),
            pl.BlockSpec((1, tl, DN_VW), lambda bb, i: (bb, n_t - 1 - i, 0)),
            pl.BlockSpec((1, 2, DN_HEADS, DN_DK, DN_DV), lambda bb, i: (bb, 0, 0, 0, 0)),
        ],
        out_shape=[
            jax.ShapeDtypeStruct((b, l, DN_VW), BF16),
            jax.ShapeDtypeStruct((b, l, DN_VW), BF16),
            jax.ShapeDtypeStruct((b, 2, DN_HEADS, DN_DK, DN_DV), F32),
        ],
        scratch_shapes=[
            pltpu.VMEM((tl + 2 * HALO, QKV_W), F32),
            pltpu.VMEM((tl + 2 * HALO, QKV_W), F32),
            pltpu.VMEM((n_t, tl, QKV_W), BF16),
            pltpu.VMEM((2, DN_HEADS, DN_DK, DN_DV), F32),
        ] + [pltpu.VMEM((2 * DN_HEADS, tl, w or tl), BF16) for w in GDN_SCRATCH.values()],
        compiler_params=_cparams(("parallel", "arbitrary")),
        name="gdn",
    )(*args)


def _rms(x, g):
    return x * lax.rsqrt(jnp.mean(x * x, -1, keepdims=True) + EPS) * g


def _rope128(x, cos, sin):
    lane = lax.broadcasted_iota(jnp.int32, x.shape, 1)
    first = (lane % (2 * AXIS_FREQS)) < AXIS_FREQS
    sw = jnp.where(first, pltpu.roll(x, LANES - AXIS_FREQS, 1), pltpu.roll(x, AXIS_FREQS, 1))
    return x * cos + sw * sin


def _mla_prep_kernel(*refs, rope, emit_cache):
    cq_ref, ckv_ref, kr_ref, qg_ref, kvg_ref, wqn_ref, wqr_ref, wkn_ref, wv_ref = refs[:9]
    k = 9
    if rope:
        cos_ref, sin_ref = refs[k:k + 2]
        k += 2
    q_out, k_out, v_out = refs[k:k + 3]
    k += 3
    scale = (MLA_NOPE + MLA_ROPE) ** -0.5
    cqn = _rms(cq_ref[0].astype(F32), qg_ref[...]).astype(BF16)
    qn = _dot(cqn, wqn_ref[...]) * scale
    qr = _dot(cqn, wqr_ref[...]) * scale
    ckvn = _rms(ckv_ref[0].astype(F32), kvg_ref[...])
    ckvn16 = ckvn.astype(BF16)
    kn = _dot(ckvn16, wkn_ref[...])
    v_out[0] = _dot(ckvn16, wv_ref[...]).T.astype(v_out.dtype)
    kr = kr_ref[0].astype(F32)
    if emit_cache:
        ckvn_out, kr_out = refs[k:k + 2]
        ckvn_out[0] = ckvn
        kr_out[0] = kr[:, :MLA_ROPE]
    if rope:
        cos = cos_ref[...]
        sin = sin_ref[...]
        kr = _rope128(kr, cos, sin)
    kr16 = kr.astype(k_out.dtype)
    for h in range(MLA_HEADS):
        qrh = qr[:, h * LANES:(h + 1) * LANES]
        if rope:
            qrh = _rope128(qrh, cos, sin)
        q_out[0, :, h * 256:h * 256 + 128] = qn[:, h * 128:(h + 1) * 128].astype(q_out.dtype)
        q_out[0, :, h * 256 + 128:(h + 1) * 256] = qrh.astype(q_out.dtype)
        k_out[0, :, h * 256:h * 256 + 128] = kn[:, h * 128:(h + 1) * 128].astype(k_out.dtype)
        k_out[0, :, h * 256 + 128:(h + 1) * 256] = kr16


def _mla_prep(y, q_norm, kv_norm, wqn, wqr, wkn, wv, layer, rope_tabs, emit_cache):
    b, l, _ = y.shape
    tm = 256
    rope = rope_tabs is not None
    w_spec = lambda shp: pl.BlockSpec((None,) + shp, lambda bb, i: (layer, 0, 0))
    in_specs = [
        pl.BlockSpec((1, tm, Q_LORA), lambda bb, i: (bb, i, CQ0 // Q_LORA)),
        pl.BlockSpec((1, tm, KV_LORA), lambda bb, i: (bb, i, CKV0 // KV_LORA)),
        pl.BlockSpec((1, tm, LANES), lambda bb, i: (bb, i, KR0 // LANES)),
        w_spec((1, Q_LORA)), w_spec((1, KV_LORA)),
        w_spec((Q_LORA, 512)), w_spec((Q_LORA, 512)), w_spec((KV_LORA, 512)), w_spec((KV_LORA, 512)),
    ]
    args = [y, y, y, q_norm, kv_norm, wqn, wqr, wkn, wv]
    if rope:
        in_specs += [pl.BlockSpec((tm, LANES), lambda bb, i: (i, 0))] * 2
        args += list(rope_tabs)
    out_specs = [
        pl.BlockSpec((1, tm, 1024), lambda bb, i: (bb, i, 0)),
        pl.BlockSpec((1, tm, 1024), lambda bb, i: (bb, i, 0)),
        pl.BlockSpec((1, 512, tm), lambda bb, i: (bb, 0, i)),
    ]
    out_shape = [
        jax.ShapeDtypeStruct((b, l, 1024), BF16),
        jax.ShapeDtypeStruct((b, l, 1024), BF16),
        jax.ShapeDtypeStruct((b, 512, l), BF16),
    ]
    if emit_cache:
        out_specs += [pl.BlockSpec((1, tm, KV_LORA), lambda bb, i: (bb, i, 0)),
                      pl.BlockSpec((1, tm, MLA_ROPE), lambda bb, i: (bb, i, 0))]
        out_shape += [jax.ShapeDtypeStruct((b, l, KV_LORA), F32),
                      jax.ShapeDtypeStruct((b, l, MLA_ROPE), F32)]
    return pl.pallas_call(
        functools.partial(_mla_prep_kernel, rope=rope, emit_cache=emit_cache),
        grid=(b, l // tm),
        in_specs=in_specs,
        out_specs=out_specs,
        out_shape=out_shape,
        compiler_params=_cparams(("parallel", "parallel")),
        name="mla_prep",
    )(*args)


def _mla_ctx_kernel(ckv_ref, kr_ref, wkn_ref, wv_ref, k_out, v_out):
    c16 = ckv_ref[0].astype(BF16)
    kn = _dot(c16, wkn_ref[...])
    v_out[0] = _dot(c16, wv_ref[...]).T.astype(v_out.dtype)
    kr16 = kr_ref[0].astype(k_out.dtype)
    zeros = jnp.zeros((kr16.shape[0], LANES - MLA_ROPE), k_out.dtype)
    for h in range(MLA_HEADS):
        k_out[0, :, h * 256:h * 256 + 128] = kn[:, h * 128:(h + 1) * 128].astype(k_out.dtype)
        k_out[0, :, h * 256 + 128:h * 256 + 128 + MLA_ROPE] = kr16
        k_out[0, :, h * 256 + 128 + MLA_ROPE:(h + 1) * 256] = zeros


def _mla_ctx(cache_ckv, cache_krope, wkn, wv, layer):
    b, _, p, _ = cache_ckv.shape
    return pl.pallas_call(
        _mla_ctx_kernel,
        grid=(b,),
        in_specs=[
            pl.BlockSpec((1, None, p, KV_LORA), lambda bb: (bb, layer, 0, 0)),
            pl.BlockSpec((1, None, p, MLA_ROPE), lambda bb: (bb, layer, 0, 0)),
            pl.BlockSpec((None, KV_LORA, 512), lambda bb: (layer, 0, 0)),
            pl.BlockSpec((None, KV_LORA, 512), lambda bb: (layer, 0, 0)),
        ],
        out_specs=[pl.BlockSpec((1, p, 1024), lambda bb: (bb, 0, 0)),
                   pl.BlockSpec((1, 512, p), lambda bb: (bb, 0, 0))],
        out_shape=[jax.ShapeDtypeStruct((b, p, 1024), BF16),
                   jax.ShapeDtypeStruct((b, 512, p), BF16)],
        compiler_params=_cparams(("parallel",)),
        name="mla_ctx",
    )(cache_ckv, cache_krope, wkn, wv)


def _attn_kernel(*refs, has_ctx, nk):
    q_ref = refs[0]
    if has_ctx:
        kc_ref, vc_ref, k_ref, v_ref, o_ref, m_ref, l_ref, acc_ref = refs[1:]
    else:
        k_ref, v_ref, o_ref, m_ref, l_ref, acc_ref = refs[1:]
    kk = pl.program_id(2)

    @pl.when(kk == 0)
    def _():
        m_ref[...] = jnp.full(m_ref.shape, -jnp.inf, F32)
        l_ref[...] = jnp.zeros(l_ref.shape, F32)
        acc_ref[...] = jnp.zeros(acc_ref.shape, F32)

    def body(kr, vtr):
        heads = range(MLA_HEADS)
        st = [_dot_nt(kr[0, :, h * 256:(h + 1) * 256], q_ref[0, :, h * 256:(h + 1) * 256]) for h in heads]
        m_prev = [m_ref[h] for h in heads]
        m_new = [jnp.maximum(m_prev[h], jnp.max(st[h], 0, keepdims=True)) for h in heads]
        alpha = [jnp.exp(m_prev[h] - m_new[h]) for h in heads]
        p = [jnp.exp(st[h] - m_new[h]) for h in heads]
        for h in heads:
            l_ref[h] = alpha[h] * l_ref[h] + jnp.sum(p[h], 0, keepdims=True)
            m_ref[h] = m_new[h]
        for h in heads:
            rows = slice(h * MLA_V, (h + 1) * MLA_V)
            acc_ref[rows, :] = alpha[h] * acc_ref[rows, :] + _dot(vtr[0, rows, :], p[h].astype(BF16))

    if has_ctx:
        @pl.when(kk == 0)
        def _():
            body(kc_ref, vc_ref)

        @pl.when(kk > 0)
        def _():
            body(k_ref, v_ref)
    else:
        body(k_ref, v_ref)

    @pl.when(kk == nk - 1)
    def _():
        for h in range(MLA_HEADS):
            hc = slice(h * MLA_V, (h + 1) * MLA_V)
            o_ref[0, :, hc] = (acc_ref[hc, :] / l_ref[h]).T.astype(o_ref.dtype)


def _attention(q, k, vt, ctx_kv, tq, tk):
    b, l, _ = q.shape
    has_ctx = ctx_kv is not None
    n_lat = l // tk
    nk = n_lat + (1 if has_ctx else 0)
    in_specs = [pl.BlockSpec((1, tq, 1024), lambda bb, i, j: (bb, i, 0))]
    args = [q]
    if has_ctx:
        assert ctx_kv[0].shape[1] == tk
        in_specs += [pl.BlockSpec((1, tk, 1024), lambda bb, i, j: (bb, 0, 0)),
                     pl.BlockSpec((1, 512, tk), lambda bb, i, j: (bb, 0, 0))]
        args += list(ctx_kv)
        kv_blk = lambda j: jnp.maximum(j - 1, 0)
    else:
        kv_blk = lambda j: j
    in_specs += [pl.BlockSpec((1, tk, 1024), lambda bb, i, j: (bb, kv_blk(j), 0)),
                 pl.BlockSpec((1, 512, tk), lambda bb, i, j: (bb, 0, kv_blk(j)))]
    args += [k, vt]
    return pl.pallas_call(
        functools.partial(_attn_kernel, has_ctx=has_ctx, nk=nk),
        grid=(b, l // tq, nk),
        in_specs=in_specs,
        out_specs=pl.BlockSpec((1, tq, 512), lambda bb, i, j: (bb, i, 0)),
        out_shape=jax.ShapeDtypeStruct((b, l, 512), BF16),
        scratch_shapes=[
            pltpu.VMEM((MLA_HEADS, 1, tq), F32),
            pltpu.VMEM((MLA_HEADS, 1, tq), F32),
            pltpu.VMEM((MLA_HEADS * MLA_V, tq), F32),
        ],
        compiler_params=_cparams(("parallel", "parallel", "arbitrary")),
        name="attn",
    )(*args)


def _merge_kernel(oa_ref, of_ref, ob_ref, dz_ref, oc_ref, gate_ref, x_ref, mod_ref, dng_ref,
                  wpa_ref, wpb_ref, wpc_ref, wout_ref, lng_ref, lnb_ref, o_ref):
    dng = dng_ref[...]
    s = of_ref[0].astype(F32) + ob_ref[0].astype(F32)
    dz = dz_ref[0].astype(F32)
    parts = []
    for h in range(DN_HEADS):
        hc = slice(h * DN_DV, (h + 1) * DN_DV)
        parts.append((_rms(s[:, hc], dng) * jax.nn.silu(dz[:, hc])).astype(BF16))
    o_dn = jnp.concatenate(parts, axis=1)
    gates = gate_ref[0]
    ga = jax.nn.sigmoid(gates[:, 0:D_MODEL].astype(F32))
    merged = ga * _dot(oa_ref[0], wpa_ref[...])
    gb = jax.nn.sigmoid(gates[:, D_MODEL:2 * D_MODEL].astype(F32))
    merged = merged + gb * _dot(o_dn, wpb_ref[...])
    gc = jax.nn.sigmoid(gates[:, 2 * D_MODEL:3 * D_MODEL].astype(F32))
    merged = merged + gc * _dot(oc_ref[0], wpc_ref[...])
    mix = _dot(merged.astype(BF16), wout_ref[...])
    g1 = mod_ref[0, 2:3, :]
    r = DN_ALPHA * x_ref[0] + g1 * mix
    o_ref[0] = _layer_norm(r, lng_ref[...], lnb_ref[...])


def _merge(o_a, o_f, o_b, y, o_c, x, mod, dn_norm, w_pa, w_pb, w_pc, w_out, ln_g, ln_b, layer):
    bm, lm, _ = x.shape
    tm = 512
    tok = lambda w, blk: pl.BlockSpec((1, tm, w), lambda b, i: (b, i, blk))
    wsp = lambda shp: pl.BlockSpec((None,) + shp, lambda b, i: (layer, 0, 0))
    return pl.pallas_call(
        _merge_kernel,
        grid=(bm, lm // tm),
        in_specs=[
            tok(512, 0), tok(512, 0), tok(512, 0), tok(512, DZ0 // 512), tok(512, 0),
            tok(3 * D_MODEL, 0), tok(D_MODEL, 0),
            pl.BlockSpec((1, 6, D_MODEL), lambda b, i: (b, 0, 0)),
            wsp((1, DN_DV)),
            wsp((A_WIDTH, D_MODEL)), wsp((DN_VW, D_MODEL)), wsp((512, D_MODEL)), wsp((D_MODEL, D_MODEL)),
            pl.BlockSpec((None, None, 1, D_MODEL), lambda b, i: (layer, 0, 0, 0)),
            pl.BlockSpec((None, None, 1, D_MODEL), lambda b, i: (layer, 0, 0, 0)),
        ],
        out_specs=tok(D_MODEL, 0),
        out_shape=jax.ShapeDtypeStruct((bm, lm, D_MODEL), F32),
        compiler_params=_cparams(("parallel", "parallel")),
        name="merge",
    )(o_a, o_f, o_b, y, o_c, y, x, mod, dn_norm, w_pa, w_pb, w_pc, w_out, ln_g, ln_b)


def _ffn_kernel(x_ref, mod_ref, wg_ref, wu_ref, wd_ref, lng_ref, lnb_ref, o_ref, h_ref, acc_ref, *, nf):
    f = pl.program_id(2)

    @pl.when(f == 0)
    def _():
        h_ref[...] = (x_ref[0] * (1.0 + mod_ref[0, 4:5, :]) + mod_ref[0, 3:4, :]).astype(BF16)
        acc_ref[...] = jnp.zeros(acc_ref.shape, F32)

    h = h_ref[...]
    act = (jax.nn.silu(_dot(h, wg_ref[...])) * _dot(h, wu_ref[...])).astype(BF16)
    acc_ref[...] += _dot(act, wd_ref[...])

    @pl.when(f == nf - 1)
    def _():
        r = DN_ALPHA * x_ref[0] + mod_ref[0, 5:6, :] * acc_ref[...]
        o_ref[0] = _layer_norm(r, lng_ref[...], lnb_ref[...])


def _ffn(x, mod, w_gu, w_down, ln_g, ln_b, layer, idx):
    bm, lm, _ = x.shape
    tm, tf = 1024, 256
    nf = D_FF // tf
    return pl.pallas_call(
        functools.partial(_ffn_kernel, nf=nf),
        grid=(bm, lm // tm, nf),
        in_specs=[
            pl.BlockSpec((1, tm, D_MODEL), lambda b, i, f: (b, i, 0)),
            pl.BlockSpec((1, 6, D_MODEL), lambda b, i, f: (b, 0, 0)),
            pl.BlockSpec((None, D_MODEL, tf), lambda b, i, f: (idx, 0, f)),
            pl.BlockSpec((None, D_MODEL, tf), lambda b, i, f: (idx, 0, nf + f)),
            pl.BlockSpec((None, tf, D_MODEL), lambda b, i, f: (idx, f, 0)),
            pl.BlockSpec((None, None, 1, D_MODEL), lambda b, i, f: (layer, 1, 0, 0)),
            pl.BlockSpec((None, None, 1, D_MODEL), lambda b, i, f: (layer, 1, 0, 0)),
        ],
        out_specs=pl.BlockSpec((1, tm, D_MODEL), lambda b, i, f: (b, i, 0)),
        out_shape=jax.ShapeDtypeStruct((bm, lm, D_MODEL), F32),
        scratch_shapes=[pltpu.VMEM((tm, D_MODEL), BF16), pltpu.VMEM((tm, D_MODEL), F32)],
        compiler_params=_cparams(("parallel", "parallel", "arbitrary")),
        name="ffn",
    )(x, mod, w_gu, w_gu, w_down, ln_g, ln_b)


def _route(h, wr, br):
    logits = _dot_hi(h, wr)
    lane = lax.broadcasted_iota(jnp.int32, logits.shape, 1)
    valid = lane < N_EXPERTS
    neg = -jnp.inf
    sel = jnp.where(valid, logits + br, neg)
    m1 = jnp.max(sel, -1, keepdims=True)
    i1 = jnp.min(jnp.where(sel == m1, lane, LANES), -1, keepdims=True)
    sel2 = jnp.where(lane == i1, neg, sel)
    m2 = jnp.max(sel2, -1, keepdims=True)
    i2 = jnp.min(jnp.where(sel2 == m2, lane, LANES), -1, keepdims=True)
    l1 = jnp.sum(jnp.where(lane == i1, logits, 0.0), -1, keepdims=True)
    l2 = jnp.sum(jnp.where(lane == i2, logits, 0.0), -1, keepdims=True)
    mx = jnp.maximum(l1, l2)
    e1 = jnp.exp(l1 - mx)
    e2 = jnp.exp(l2 - mx)
    den = e1 + e2
    comb = jnp.where(lane == i1, e1 / den, 0.0) + jnp.where(lane == i2, e2 / den, 0.0)
    return comb, (lane == i1) | (lane == i2)


MOE_TM = 1024
MOE_CAP0 = 256
MOE_TAIL = 32
MOE_CAP1 = 256


def _moe_kernel(x_ref, mod_ref, wr_ref, br_ref, tri_ref, wg_ref, wu_ref, wd_ref, lng_ref, lnb_ref, o_ref,
                h_ref, comb_ref, slot_ref, slot_t_ref, acc_ref, ytail_ref):
    e = pl.program_id(2)
    tm = h_ref.shape[0]
    always = MOE_CAP0 + MOE_TAIL

    @pl.when(e == 0)
    def _():
        hf = x_ref[0] * (1.0 + mod_ref[0, 4:5, :]) + mod_ref[0, 3:4, :]
        h_ref[...] = hf.astype(BF16)
        comb, sel = _route(hf, wr_ref[...], br_ref[...])
        comb_ref[...] = comb
        sel16 = jnp.where(sel, 1.0, 0.0).astype(BF16)
        nb = tri_ref.shape[0]
        carry = jnp.zeros((1, LANES), F32)
        for blk in range(tm // nb):
            rows = slice(blk * nb, (blk + 1) * nb)
            rank = _dot(tri_ref[...], sel16[rows]) + carry
            slot_ref[rows, :] = jnp.where(sel[rows], rank, -1.0)
            carry = carry + jnp.sum(sel16[rows].astype(F32), 0, keepdims=True)
        slot_t_ref[...] = slot_ref[...].T
        acc_ref[...] = jnp.zeros(acc_ref.shape, F32)

    lane = lax.broadcasted_iota(jnp.int32, comb_ref.shape, 1)
    pick = lane == e
    c_col = jnp.sum(jnp.where(pick, comb_ref[...], 0.0), -1, keepdims=True)
    slot_col = jnp.max(jnp.where(pick, slot_ref[...], -1.0), -1, keepdims=True)
    slot_row = slot_t_ref[pl.ds(e, 1), :]
    load = jnp.max(slot_row) + 1.0

    def expert(base, cap):
        r_i = lax.broadcasted_iota(jnp.int32, (cap, tm), 0).astype(F32) + base
        gather = jnp.where(r_i == slot_row, 1.0, 0.0).astype(BF16)
        xe = _dot(gather, h_ref[...]).astype(BF16)
        act = (jax.nn.silu(_dot(xe, wg_ref[...])) * _dot(xe, wu_ref[...])).astype(BF16)
        return gather, _dot(act, wd_ref[...])

    def scatter(base, ye16):
        c_i = lax.broadcasted_iota(jnp.int32, (tm, ye16.shape[0]), 1).astype(F32) + base
        onehot = jnp.where(c_i == slot_col, 1.0, 0.0).astype(BF16)
        acc_ref[...] += c_col * _dot(onehot, ye16)

    def block(base, cap):
        _, ye = expert(base, cap)
        scatter(base, ye.astype(BF16))

    gather, ye = expert(0.0, always)
    scatter(0.0, ye[:MOE_CAP0].astype(BF16))
    c_b = jnp.broadcast_to(c_col, (tm, LANES))
    g_tail = gather[MOE_CAP0:]
    c_slot = sum(_dot(g_tail, piece) for piece in _split3(c_b))
    ytail_ref[pl.ds(pl.multiple_of(e * MOE_TAIL, MOE_TAIL), MOE_TAIL), :] = (
        ye[MOE_CAP0:] * c_slot[:, 0:1]).astype(BF16)
    base = always
    while base < tm:
        pl.when(load > base)(functools.partial(block, float(base), MOE_CAP1))
        base += MOE_CAP1

    @pl.when(e == N_EXPERTS - 1)
    def _():
        n_tail = N_EXPERTS * MOE_TAIL
        slot = slot_ref[...]
        in_tail = (slot >= MOE_CAP0) & (slot < always)
        tslot = jnp.where(in_tail, slot - MOE_CAP0, 2.0 * MOE_TAIL).astype(BF16)
        e_i = lax.broadcasted_iota(jnp.int32, (LANES, n_tail), 0)
        l_i = lax.broadcasted_iota(jnp.int32, (LANES, n_tail), 1)
        expand = jnp.where(l_i // MOE_TAIL == e_i, 1.0, 0.0).astype(BF16)
        want = (lax.broadcasted_iota(jnp.int32, (tm, n_tail), 1) % MOE_TAIL).astype(F32)
        onehot = jnp.where(_dot(tslot, expand) == want, 1.0, 0.0).astype(BF16)
        ff = acc_ref[...] + _dot(onehot, ytail_ref[...])
        r = DN_ALPHA * x_ref[0] + mod_ref[0, 5:6, :] * ff
        o_ref[0] = _layer_norm(r, lng_ref[...], lnb_ref[...])


def _moe(x, mod, w_router_p, b_router_p, tri, w_gu, w_down, ln_g, ln_b, layer, idx):
    bm, lm, _ = x.shape
    tm = MOE_TM
    nb = tri.shape[0]
    return pl.pallas_call(
        _moe_kernel,
        grid=(bm, lm // tm, N_EXPERTS),
        in_specs=[
            pl.BlockSpec((1, tm, D_MODEL), lambda b, i, e: (b, i, 0)),
            pl.BlockSpec((1, 6, D_MODEL), lambda b, i, e: (b, 0, 0)),
            pl.BlockSpec((None, D_MODEL, LANES), lambda b, i, e: (idx, 0, 0)),
            pl.BlockSpec((None, 1, LANES), lambda b, i, e: (idx, 0, 0)),
            pl.BlockSpec((nb, nb), lambda b, i, e: (0, 0)),
            pl.BlockSpec((None, None, D_MODEL, D_FF_EXPERT), lambda b, i, e: (idx, e, 0, 0)),
            pl.BlockSpec((None, None, D_MODEL, D_FF_EXPERT), lambda b, i, e: (idx, e, 0, 1)),
            pl.BlockSpec((None, None, D_FF_EXPERT, D_MODEL), lambda b, i, e: (idx, e, 0, 0)),
            pl.BlockSpec((None, None, 1, D_MODEL), lambda b, i, e: (layer, 1, 0, 0)),
            pl.BlockSpec((None, None, 1, D_MODEL), lambda b, i, e: (layer, 1, 0, 0)),
        ],
        out_specs=pl.BlockSpec((1, tm, D_MODEL), lambda b, i, e: (b, i, 0)),
        out_shape=jax.ShapeDtypeStruct((bm, lm, D_MODEL), F32),
        scratch_shapes=[pltpu.VMEM((tm, D_MODEL), BF16), pltpu.VMEM((tm, LANES), F32),
                        pltpu.VMEM((tm, LANES), F32), pltpu.VMEM((LANES, tm), F32),
                        pltpu.VMEM((tm, D_MODEL), F32), pltpu.VMEM((N_EXPERTS * MOE_TAIL, D_MODEL), BF16)],
        compiler_params=_cparams(("parallel", "parallel", "arbitrary")),
        name="moe",
    )(x, mod, w_router_p, b_router_p, tri, w_gu, w_gu, w_down, ln_g, ln_b)


def _pack_w_in(w_in):
    d = w_in.shape[0]
    w_in = w_in.astype(BF16)
    o = 0
    cols = {}
    for name, width in (("a_u", A_WIDTH), ("a_v", A_WIDTH), ("dq", DN_QK), ("dk", DN_QK), ("dv", DN_VW),
                        ("dz", DN_VW), ("sm", 4 * DN_HEADS), ("cq", Q_LORA), ("ckv", KV_LORA), ("kr", MLA_ROPE),
                        ("ga", D_MODEL), ("gb", D_MODEL), ("gc", D_MODEL)):
        cols[name] = w_in[:, :, o:o + width]
        o += width
    zpad = lambda n: jnp.zeros((d, D_MODEL, n), w_in.dtype)
    packed = jnp.concatenate(
        [cols["ga"], cols["gb"], cols["gc"], cols["dq"], cols["dk"], cols["dv"], cols["dz"],
         cols["a_u"], cols["a_v"], cols["cq"], cols["ckv"],
         cols["kr"], zpad(LANES - MLA_ROPE), cols["sm"], zpad(LANES - 4 * DN_HEADS)], axis=-1)
    return packed


def _rope_tables(n_tok):
    row = (jnp.arange(n_tok) // GRID_W).astype(F32)
    col = (jnp.arange(n_tok) % GRID_W).astype(F32)
    inv_freq = ROPE_BASE ** (-jnp.arange(AXIS_FREQS, dtype=F32) / AXIS_FREQS)
    ang_r = row[:, None] * inv_freq
    ang_c = col[:, None] * inv_freq
    ones = jnp.ones((n_tok, LANES - MLA_ROPE), F32)
    cos = jnp.concatenate([jnp.cos(ang_r), jnp.cos(ang_r), jnp.cos(ang_c), jnp.cos(ang_c), ones], axis=1)
    sin = jnp.concatenate([-jnp.sin(ang_r), jnp.sin(ang_r), -jnp.sin(ang_c), jnp.sin(ang_c), 0.0 * ones], axis=1)
    return cos, sin


def kernel(x_prompt, x_sample, state_dn, cache_ckv, cache_krope, c, c_ctx, w_mod, b_mod, w_in, a_ln_g, a_ln_b, a_ws, a_bs, dn_conv, dn_a_log, dn_dt_bias, dn_norm, q_norm, w_qb, kv_norm, w_kvb, w_pa, w_pb, w_pc, w_out, ln_g, ln_b, ffn_gu, ffn_down, moe_router, moe_bias, moe_gu, moe_down):
    batch, seq, _ = x_prompt.shape
    dec_batch, dec_seq, _ = x_sample.shape
    depth = w_in.shape[0]
    assert depth == DEPTH

    w_in_p = _pack_w_in(w_in)
    a_bs_t = jnp.swapaxes(a_bs, 1, 2)
    ln_a_g = a_ln_g.reshape(depth, 1, A_WIDTH)
    ln_a_b = a_ln_b.reshape(depth, 1, A_WIDTH)
    lane_pad = lambda v: jnp.pad(v.reshape(depth, 1, -1), ((0, 0), (0, 0), (0, LANES - v[0].size)))
    alog_row = lane_pad(dn_a_log)
    dtb_row = lane_pad(dn_dt_bias)
    dn_g = dn_norm.reshape(depth, 1, DN_DV)
    qg = q_norm.reshape(depth, 1, Q_LORA)
    kvg = kv_norm.reshape(depth, 1, KV_LORA)
    wq = w_qb.reshape(depth, Q_LORA, MLA_HEADS, MLA_NOPE + MLA_ROPE)
    wqn = wq[..., :MLA_NOPE].reshape(depth, Q_LORA, MLA_HEADS * MLA_NOPE).astype(BF16)
    wqr = jnp.pad(wq[..., MLA_NOPE:], ((0, 0), (0, 0), (0, 0), (0, LANES - MLA_ROPE)))
    wqr = wqr.reshape(depth, Q_LORA, MLA_HEADS * LANES).astype(BF16)
    wkv = w_kvb.reshape(depth, KV_LORA, MLA_HEADS, MLA_NOPE + MLA_V)
    wkn = wkv[..., :MLA_NOPE].reshape(depth, KV_LORA, MLA_HEADS * MLA_NOPE).astype(BF16)
    wv = wkv[..., MLA_NOPE:].reshape(depth, KV_LORA, MLA_HEADS * MLA_V).astype(BF16)
    w_pa16, w_pb16, w_pc16, w_out16 = (w.astype(BF16) for w in (w_pa, w_pb, w_pc, w_out))
    ffn_gu16, ffn_down16 = ffn_gu.astype(BF16), ffn_down.astype(BF16)
    moe_gu16, moe_down16 = moe_gu.astype(BF16), moe_down.astype(BF16)
    n_moe = moe_router.shape[0]
    wr_p = jnp.pad(moe_router, ((0, 0), (0, 0), (0, LANES - N_EXPERTS)))
    br_p = jnp.pad(moe_bias.reshape(n_moe, 1, N_EXPERTS), ((0, 0), (0, 0), (0, LANES - N_EXPERTS)))
    ln_g4 = ln_g.reshape(depth, 2, 1, D_MODEL)
    ln_b4 = ln_b.reshape(depth, 2, 1, D_MODEL)
    rope_tabs = _rope_tables(dec_seq)
    gdn_tabs = _gdn_tables(GDN_TILE)
    moe_tri = (jnp.arange(256)[None, :] < jnp.arange(256)[:, None]).astype(BF16)

    n_cond = 16
    cond = jnp.concatenate([c, c_ctx[None, :], jnp.zeros((n_cond - dec_batch - 1, D_MODEL), F32)], axis=0)
    mods = _adaln(cond, w_mod, b_mod).reshape(depth, n_cond, 6, D_MODEL)

    xp = x_prompt.reshape(1, batch * seq, D_MODEL)
    xs = x_sample
    st_dn, st_ckv, st_kr = [], [], []

    def channel_mixer(x, mod, l):
        if l % 2 == 0:
            return _ffn(x, mod, ffn_gu16, ffn_down16, ln_g4, ln_b4, l, l // 2)
        return _moe(x, mod, wr_p, br_p, moe_tri, moe_gu16, moe_down16, ln_g4, ln_b4, l, l // 2)

    def token_mixer(x, mod, l, bsz, n_tok, latent):
        y = _inproj(x, mod, w_in_p, l)
        o_a = _gmlp(y.reshape(-1, NP), a_ws, a_bs_t, ln_a_g, ln_a_b, l).reshape(x.shape[0], x.shape[1], A_WIDTH)
        ys = y.reshape(bsz, n_tok, NP)
        o_f, o_b, s_fin = _gdn(ys, dn_conv, alog_row, dtb_row, gdn_tabs, state_dn if latent else None, l)
        if latent:
            q, k, v = _mla_prep(ys, qg, kvg, wqn, wqr, wkn, wv, l, rope_tabs, False)
            ctx_kv = _mla_ctx(cache_ckv, cache_krope, wkn, wv, l)
            o_c = _attention(q, k, v, ctx_kv, 512, 512)
            extra = None
        else:
            q, k, v, ckv_n, kr = _mla_prep(ys, qg, kvg, wqn, wqr, wkn, wv, l, None, True)
            o_c = _attention(q, k, v, None, n_tok, n_tok)
            extra = (s_fin, ckv_n, kr)
        shp = (x.shape[0], x.shape[1], 512)
        x = _merge(o_a, o_f.reshape(shp), o_b.reshape(shp), y, o_c.reshape(shp), x, mod, dn_g,
                   w_pa16, w_pb16, w_pc16, w_out16, ln_g4, ln_b4, l)
        return x, extra

    for l in range(depth):
        mod_p = mods[l, dec_batch:dec_batch + 1]
        mod_s = mods[l, :dec_batch]
        xp, (s_fin, ckv_n, kr) = token_mixer(xp, mod_p, l, batch, seq, False)
        xp = channel_mixer(xp, mod_p, l)
        st_dn.append(s_fin)
        st_ckv.append(ckv_n)
        st_kr.append(kr)
        xs, _ = token_mixer(xs, mod_s, l, dec_batch, dec_seq, True)
        xs = channel_mixer(xs, mod_s, l)

    return (xp.reshape(batch, seq, D_MODEL), xs,
            jnp.stack(st_dn, axis=1), jnp.stack(st_ckv, axis=1), jnp.stack(st_kr, axis=1))
```
